```python
import math
import jax, jax.numpy as jnp
from jax import lax
import numpy as np

D_MODEL = 1024
BATCH = 4
SEQ = 8192
DEPTH = 2

PLE_DIM = 256
MIX_WIDTH = D_MODEL
HEAD_DIM = 64
LRU_WIDTH = D_MODEL // 4
LRU_HEADS = LRU_WIDTH // HEAD_DIM
LRU_BLOCK = LRU_WIDTH // LRU_HEADS
LRU_C = 8.0
CONV_K = 4
ATT_WIDTH = D_MODEL // 4
ATT_HEADS = ATT_WIDTH // HEAD_DIM
ATT_BLOCK = 128
SSD_WIDTH = D_MODEL // 2
SSD_HEADS = SSD_WIDTH // HEAD_DIM
SSD_GROUPS = 2
SSD_HEADS_PER_GROUP = SSD_HEADS // SSD_GROUPS
SSD_STATE = 128
SSD_CHUNK = 128
SSD_CONV_DIM = SSD_WIDTH + 2 * SSD_GROUPS * SSD_STATE
FFN_DIM = 128 * ((8 * D_MODEL // 3 + 127) // 128)
ALPHA = (2.0 * DEPTH) ** 0.25
BETA = (8.0 * DEPTH) ** -0.25
LN_EPS = 1e-5
RMS_EPS = 1e-5
IN_SIZES = (LRU_WIDTH, LRU_WIDTH,
            ATT_WIDTH, ATT_WIDTH, ATT_WIDTH,
            ATT_HEADS,
            SSD_WIDTH, SSD_CONV_DIM, SSD_HEADS)
IN_WIDTH = sum(IN_SIZES)

kernel_name = "hymba_style_lru_fox_ssd_macaron_deepnorm"


def _split_points(sizes):
    pts, acc = [], 0
    for s in sizes[:-1]:
        acc += s
        pts.append(acc)
    return pts


def layer_norm(x, g, b):
    xf = x.astype(jnp.float32)
    mu = jnp.mean(xf, axis=-1, keepdims=True)
    var = jnp.mean(jnp.square(xf - mu), axis=-1, keepdims=True)
    y = (xf - mu) * lax.rsqrt(var + LN_EPS) * g.astype(jnp.float32) + b.astype(jnp.float32)
    return y.astype(x.dtype)


def swiglu(x, wg, wu, wd):
    return (jax.nn.silu(x @ wg) * (x @ wu)) @ wd


def causal_dwconv(u, w, b):
    c = u.shape[-1]
    y = lax.conv_general_dilated(u, w[:, None, :].astype(u.dtype), window_strides=(1,),
                                 padding=[(CONV_K - 1, 0)],
                                 dimension_numbers=("NWC", "WIO", "NWC"),
                                 feature_group_count=c)
    return y + b


def _lin_combine(e1, e2):
    a1, b1 = e1
    a2, b2 = e2
    return a1 * a2, a2 * b1 + b2


def rglru_block(u_raw, gate_raw, conv_w, conv_b, wa, ba, wx, bx, lam):
    bsz, s, _ = u_raw.shape
    u = causal_dwconv(u_raw, conv_w, conv_b)
    ub = u.reshape(bsz, s, LRU_HEADS, LRU_BLOCK)
    r = jax.nn.sigmoid(jnp.einsum("bshi,hij->bshj", ub, wa).reshape(bsz, s, LRU_WIDTH) + ba)
    ig = jax.nn.sigmoid(jnp.einsum("bshi,hij->bshj", ub, wx).reshape(bsz, s, LRU_WIDTH) + bx)
    log_a = -LRU_C * r.astype(jnp.float32) * jax.nn.softplus(-lam.astype(jnp.float32))
    a = jnp.exp(log_a)
    b = jnp.sqrt(-jnp.expm1(2.0 * log_a)) * (ig * u).astype(jnp.float32)
    _, h = lax.associative_scan(_lin_combine, (a, b), axis=1)
    return (h * jax.nn.gelu(gate_raw.astype(jnp.float32))).astype(u_raw.dtype)


def forgetting_attention(q, k, v, f_logit, b_f):
    bsz, s, _ = q.shape
    nblk = s // ATT_BLOCK
    log_f = jax.nn.log_sigmoid((f_logit + b_f).astype(jnp.float32))
    F = jnp.cumsum(log_f, axis=1).transpose(0, 2, 1)
    qh = q.astype(jnp.float32).reshape(bsz, s, ATT_HEADS, HEAD_DIM).transpose(0, 2, 1, 3) * (HEAD_DIM ** -0.5)
    kh = k.astype(jnp.float32).reshape(bsz, s, ATT_HEADS, HEAD_DIM).transpose(0, 2, 1, 3)
    vh = v.astype(jnp.float32).reshape(bsz, s, ATT_HEADS, HEAD_DIM).transpose(0, 2, 1, 3)
    q_blocks = qh.reshape(bsz, ATT_HEADS, nblk, ATT_BLOCK, HEAD_DIM).transpose(2, 0, 1, 3, 4)
    f_blocks = F.reshape(bsz, ATT_HEADS, nblk, ATT_BLOCK).transpose(2, 0, 1, 3)
    key_pos = jnp.arange(s)

    def one_block(args):
        qb, fb, bi = args
        logits = jnp.einsum("bhqd,bhkd->bhqk", qb, kh) + fb[..., :, None] - F[:, :, None, :]
        q_pos = bi * ATT_BLOCK + jnp.arange(ATT_BLOCK)
        logits = jnp.where(key_pos[None, :] <= q_pos[:, None], logits, -jnp.inf)
        w = jax.nn.softmax(logits, axis=-1)
        return jnp.einsum("bhqk,bhkd->bhqd", w, vh)

    out = lax.map(one_block, (q_blocks, f_blocks, jnp.arange(nblk)))
    out = out.transpose(1, 0, 3, 2, 4).reshape(bsz, s, ATT_WIDTH)
    return out.astype(q.dtype)


def segsum(x):
    t = x.shape[-1]
    xc = jnp.cumsum(x, axis=-1)
    seg = xc[..., :, None] - xc[..., None, :]
    mask = jnp.tril(jnp.ones((t, t), dtype=bool))
    return jnp.where(mask, seg, -jnp.inf)


def ssd_chunked(xs, a, bm, cm):
    bsz, s, h, p = xs.shape
    n = bm.shape[-1]
    c = s // SSD_CHUNK
    xs = xs.reshape(bsz, c, SSD_CHUNK, h, p)
    bm = bm.reshape(bsz, c, SSD_CHUNK, h, n)
    cm = cm.reshape(bsz, c, SSD_CHUNK, h, n)
    a = a.reshape(bsz, c, SSD_CHUNK, h).transpose(0, 3, 1, 2)
    a_cum = jnp.cumsum(a, axis=-1)
    L = jnp.exp(segsum(a))
    y_diag = jnp.einsum("bclhn,bcshn,bhcls,bcshp->bclhp", cm, bm, L, xs)
    decay_states = jnp.exp(a_cum[..., -1:] - a_cum)
    states = jnp.einsum("bclhn,bhcl,bclhp->bchpn", bm, decay_states, xs)
    states = jnp.concatenate([jnp.zeros_like(states[:, :1]), states], axis=1)
    decay_chunk = jnp.exp(segsum(jnp.pad(a_cum[..., -1], ((0, 0), (0, 0), (1, 0)))))
    states = jnp.einsum("bhzc,bchpn->bzhpn", decay_chunk, states)[:, :-1]
    y_off = jnp.einsum("bclhn,bchpn,bhcl->bclhp", cm, states, jnp.exp(a_cum))
    return (y_diag + y_off).reshape(bsz, s, h, p)


def ssd_mixer(z, xbc_raw, dt_raw, conv_w, conv_b, dt_bias, a_log, d_skip, norm_g):
    bsz, s, _ = z.shape
    xbc = jax.nn.silu(causal_dwconv(xbc_raw, conv_w, conv_b)).astype(jnp.float32)
    xs, bm, cm = jnp.split(xbc, [SSD_WIDTH, SSD_WIDTH + SSD_GROUPS * SSD_STATE], axis=-1)
    dt = jax.nn.softplus((dt_raw + dt_bias).astype(jnp.float32))
    A = -jnp.exp(a_log.astype(jnp.float32))
    xs_h = xs.reshape(bsz, s, SSD_HEADS, HEAD_DIM)
    bm_h = jnp.repeat(bm.reshape(bsz, s, SSD_GROUPS, SSD_STATE), SSD_HEADS_PER_GROUP, axis=2)
    cm_h = jnp.repeat(cm.reshape(bsz, s, SSD_GROUPS, SSD_STATE), SSD_HEADS_PER_GROUP, axis=2)
    y = ssd_chunked(xs_h * dt[..., None], A * dt, bm_h, cm_h)
    y = y + d_skip.astype(jnp.float32)[:, None] * xs_h
    y = y.reshape(bsz, s, SSD_WIDTH) * jax.nn.silu(z.astype(jnp.float32))
    yg = y.reshape(bsz, s, SSD_GROUPS, SSD_WIDTH // SSD_GROUPS)
    yg = yg * lax.rsqrt(jnp.mean(jnp.square(yg), axis=-1, keepdims=True) + RMS_EPS)
    y = yg.reshape(bsz, s, SSD_WIDTH) * norm_g.astype(jnp.float32)
    return y.astype(z.dtype)


def setup_inputs(seed: int = 0) -> dict:
    key = jax.random.key(seed)
    ks = iter(jax.random.split(key, 33))
    f32 = jnp.float32

    def nrm(shape, scale):
        return scale * jax.random.normal(next(ks), shape, f32)

    def gain(shape):
        return 1.0 + nrm(shape, 0.02)

    d = DEPTH
    out = {}
    out["x"] = nrm((BATCH, SEQ, D_MODEL), 1.0)
    out["p"] = nrm((DEPTH, BATCH, SEQ, PLE_DIM), 1.0)
    out["ln1_g"] = gain((d, D_MODEL))
    out["ln1_b"] = nrm((d, D_MODEL), 0.02)
    out["ffn1_wg"] = nrm((d, D_MODEL, FFN_DIM), BETA * D_MODEL ** -0.5)
    out["ffn1_wu"] = nrm((d, D_MODEL, FFN_DIM), BETA * D_MODEL ** -0.5)
    out["ffn1_wd"] = nrm((d, FFN_DIM, D_MODEL), BETA * FFN_DIM ** -0.5)
    out["w_in"] = nrm((d, D_MODEL, IN_WIDTH), D_MODEL ** -0.5)
    out["lru_conv_w"] = nrm((d, CONV_K, LRU_WIDTH), CONV_K ** -0.5)
    out["lru_conv_b"] = nrm((d, LRU_WIDTH), 0.02)
    out["lru_wa"] = nrm((d, LRU_HEADS, LRU_BLOCK, LRU_BLOCK), LRU_BLOCK ** -0.5)
    out["lru_ba"] = nrm((d, LRU_WIDTH), 0.02)
    out["lru_wx"] = nrm((d, LRU_HEADS, LRU_BLOCK, LRU_BLOCK), LRU_BLOCK ** -0.5)
    out["lru_bx"] = nrm((d, LRU_WIDTH), 0.02)
    a_c = jax.random.uniform(next(ks), (d, LRU_WIDTH), f32, 0.9, 0.999)
    a0 = a_c ** (1.0 / LRU_C)
    out["lru_lambda"] = jnp.log(a0) - jnp.log1p(-a0)
    out["fox_bf"] = 3.0 + nrm((d, ATT_HEADS), 0.1)
    out["ssd_conv_w"] = nrm((d, CONV_K, SSD_CONV_DIM), CONV_K ** -0.5)
    out["ssd_conv_b"] = nrm((d, SSD_CONV_DIM), 0.02)
    dt0 = jnp.exp(jax.random.uniform(next(ks), (d, SSD_HEADS), f32, math.log(1e-3), math.log(1e-1)))
    out["ssd_dt_bias"] = dt0 + jnp.log(-jnp.expm1(-dt0))
    out["ssd_a_log"] = jnp.log(jax.random.uniform(next(ks), (d, SSD_HEADS), f32, 1.0, 16.0))
    out["ssd_d"] = gain((d, SSD_HEADS))
    out["ssd_norm_g"] = gain((d, SSD_WIDTH))
    out["w_out"] = nrm((d, MIX_WIDTH, D_MODEL), BETA * MIX_WIDTH ** -0.5)
    out["ln2_g"] = gain((d, D_MODEL))
    out["ln2_b"] = nrm((d, D_MODEL), 0.02)
    out["ffn2_wg"] = nrm((d, D_MODEL, FFN_DIM), BETA * D_MODEL ** -0.5)
    out["ffn2_wu"] = nrm((d, D_MODEL, FFN_DIM), BETA * D_MODEL ** -0.5)
    out["ffn2_wd"] = nrm((d, FFN_DIM, D_MODEL), BETA * FFN_DIM ** -0.5)
    out["ln3_g"] = gain((d, D_MODEL))
    out["ln3_b"] = nrm((d, D_MODEL), 0.02)
    out["pe_proj"] = nrm((d, PLE_DIM, D_MODEL), BETA * PLE_DIM ** -0.5)
    out["pe_gate_w"] = nrm((d, D_MODEL, D_MODEL), D_MODEL ** -0.5)
    out["pe_gate_b"] = nrm((d, D_MODEL), 0.02)
    return out


def reference(x, p, ln1_g, ln1_b, ffn1_wg, ffn1_wu, ffn1_wd, w_in,
              lru_conv_w, lru_conv_b, lru_wa, lru_ba, lru_wx, lru_bx, lru_lambda,
              fox_bf, ssd_conv_w, ssd_conv_b, ssd_dt_bias, ssd_a_log, ssd_d, ssd_norm_g,
              w_out, ln2_g, ln2_b, ffn2_wg, ffn2_wu, ffn2_wd, ln3_g, ln3_b,
              pe_proj, pe_gate_w, pe_gate_b):
    pts = _split_points(IN_SIZES)
    for i in range(DEPTH):
        x = layer_norm(ALPHA * x + 0.5 * swiglu(x, ffn1_wg[i], ffn1_wu[i], ffn1_wd[i]), ln1_g[i], ln1_b[i])
        h = x @ w_in[i]
        u_lru, g_lru, q, k, v, f_logit, z, xbc, dt_raw = jnp.split(h, pts, axis=-1)
        y_a = rglru_block(u_lru, g_lru, lru_conv_w[i], lru_conv_b[i], lru_wa[i], lru_ba[i],
                          lru_wx[i], lru_bx[i], lru_lambda[i])
        y_b = forgetting_attention(q, k, v, f_logit, fox_bf[i])
        y_c = ssd_mixer(z, xbc, dt_raw, ssd_conv_w[i], ssd_conv_b[i], ssd_dt_bias[i],
                        ssd_a_log[i], ssd_d[i], ssd_norm_g[i])
        mix = jnp.concatenate([y_a, y_b, y_c], axis=-1) @ w_out[i]
        x = layer_norm(ALPHA * x + mix, ln2_g[i], ln2_b[i])
        x = layer_norm(ALPHA * x + 0.5 * swiglu(x, ffn2_wg[i], ffn2_wu[i], ffn2_wd[i]), ln3_g[i], ln3_b[i])
        x = x + jax.nn.sigmoid(x @ pe_gate_w[i] + pe_gate_b[i]) * (p[i] @ pe_proj[i])
    return x
```

```python
import functools
import math

import jax
import jax.numpy as jnp
from jax import lax
from jax.experimental import pallas as pl
from jax.experimental.pallas import tpu as pltpu

F32 = jnp.float32
BF16 = jnp.bfloat16

LANES = 128
HEAD_DIM = 64
LRU_C = 8.0
CONV_K = 4
SSD_CHUNK = 128
SSD_STATE = 128
SSD_HEADS = 8
SSD_GROUPS = 2
ATT_HEADS = 4
LN_EPS = 1e-5
RMS_EPS = 1e-5
VMEM_LIMIT = 56 * 1024 * 1024


def _cparams(*sem):
    return pltpu.CompilerParams(dimension_semantics=sem, vmem_limit_bytes=VMEM_LIMIT)


def _resident(shape):
    nd = len(shape)
    return pl.BlockSpec(shape, lambda *_: (0,) * nd, pipeline_mode=pl.Buffered(1))


def _layer_norm(y, g, b):
    mu = jnp.mean(y, axis=-1, keepdims=True)
    d = y - mu
    var = jnp.mean(d * d, axis=-1, keepdims=True)
    return d * lax.rsqrt(var + LN_EPS) * g + b


def _dot(a, b):
    return jnp.dot(a, b, preferred_element_type=F32)


def _dot_exact(a, b):
    return jnp.dot(a, b, preferred_element_type=F32, precision=lax.Precision.HIGHEST)


def _ffn_ln_kernel(x_ref, wg_ref, wu_ref, wd_ref, g_ref, b_ref, o_ref, *, alpha, fchunk):
    x = x_ref[...]
    xb = x.astype(BF16)
    ffn = wg_ref.shape[1]
    acc = jnp.zeros(x.shape, F32)
    for c0 in range(0, ffn, fchunk):
        c1 = min(c0 + fchunk, ffn)
        g = _dot(xb, wg_ref[:, c0:c1])
        u = _dot(xb, wu_ref[:, c0:c1])
        h = (g * jax.nn.sigmoid(g) * u).astype(BF16)
        acc = acc + _dot(h, wd_ref[c0:c1, :])
    o_ref[...] = _layer_norm(alpha * x + 0.5 * acc, g_ref[...], b_ref[...])


def _ffn_ln(x, wg, wu, wd, g, b, *, alpha, tm=512, fchunk=512):
    t, d = x.shape
    row = pl.BlockSpec((tm, d), lambda i: (i, 0))
    return pl.pallas_call(
        functools.partial(_ffn_ln_kernel, alpha=alpha, fchunk=fchunk),
        out_shape=jax.ShapeDtypeStruct((t, d), F32),
        grid=(t // tm,),
        in_specs=[row, _resident(wg.shape), _resident(wu.shape), _resident(wd.shape),
                  _resident(g.shape), _resident(b.shape)],
        out_specs=row,
        compiler_params=_cparams("parallel"),
        name="ffn_ln",
    )(x, wg, wu, wd, g, b)


_IN_UG, _IN_QKV, _IN_Z, _IN_XBC, _IN_SMALL = 512, 768, 512, 1024, LANES


def _in_proj_kernel(x_ref, w_ref, ug_ref, q_ref, k_ref, ve_ref, vo_ref, z_ref, xbc_ref, sm_ref):
    h = _dot(x_ref[...].astype(BF16), w_ref[...])
    o = 0
    ug_ref[...] = h[:, o:o + _IN_UG]
    o += _IN_UG
    aw = _IN_QKV // 3
    q = h[:, o:o + aw]
    k = h[:, o + aw:o + 2 * aw]
    v = h[:, o + 2 * aw:o + 3 * aw]
    o += _IN_QKV
    q_ref[...] = (q * (HEAD_DIM ** -0.5)).astype(BF16)
    k_ref[...] = k.astype(BF16)
    even = (lax.broadcasted_iota(jnp.int32, v.shape, 1) % LANES) < HEAD_DIM
    ve_ref[...] = jnp.where(even, v, 1.0).astype(BF16)
    vo_ref[...] = jnp.where(even, 1.0, v).astype(BF16)
    z_ref[...] = h[:, o:o + _IN_Z]
    o += _IN_Z
    xbc_ref[...] = h[:, o:o + _IN_XBC]
    o += _IN_XBC
    sm_ref[...] = h[:, o:o + _IN_SMALL]


def _in_proj(x, w, *, tm=512):
    t, d = x.shape
    aw = _IN_QKV // 3

    def row(n):
        return pl.BlockSpec((tm, n), lambda i: (i, 0))

    outs = [(_IN_UG, F32), (aw, BF16), (aw, BF16), (aw, BF16), (aw, BF16),
            (_IN_Z, F32), (_IN_XBC, F32), (_IN_SMALL, F32)]
    return pl.pallas_call(
        _in_proj_kernel,
        out_shape=[jax.ShapeDtypeStruct((t, n), dt) for n, dt in outs],
        grid=(t // tm,),
        in_specs=[row(d), _resident(w.shape)],
        out_specs=[row(n) for n, _ in outs],
        compiler_params=_cparams("parallel"),
        name="in_proj",
    )(x, w)


_TAIL = 8


def _causal_conv(buf_ref, x, w, b, first):
    ts = x.shape[0]

    @pl.when(first)
    def _():
        buf_ref[0:_TAIL, :] = jnp.zeros((_TAIL, x.shape[1]), F32)

    buf_ref[_TAIL:_TAIL + ts, :] = x
    y = b + w[CONV_K - 1:CONV_K, :] * x
    for j in range(CONV_K - 1):
        off = _TAIL - (CONV_K - 1) + j
        y = y + w[j:j + 1, :] * buf_ref[off:off + ts, :]
    buf_ref[0:_TAIL, :] = x[ts - _TAIL:ts, :]
    return y


def _lower_tri(n):
    r = lax.broadcasted_iota(jnp.int32, (n, n), 0)
    c = lax.broadcasted_iota(jnp.int32, (n, n), 1)
    return r >= c


def _lru_kernel(ug_ref, cw_ref, cb_ref, wa_ref, ba_ref, wx_ref, bx_ref, lam_ref, o_ref,
                buf_ref, h_ref):
    first = pl.program_id(1) == 0
    w = ug_ref.shape[1] // 2
    ts = ug_ref.shape[0]
    u_raw = ug_ref[:, 0:w]
    gate = ug_ref[:, w:2 * w]
    u = _causal_conv(buf_ref, u_raw, cw_ref[...], cb_ref[...], first)
    ub = u.astype(BF16)
    r = jax.nn.sigmoid(_dot(ub, wa_ref[...]) + ba_ref[...])
    ig = jax.nn.sigmoid(_dot(ub, wx_ref[...]) + bx_ref[...])
    log_a = -LRU_C * r * jax.nn.softplus(-lam_ref[...])
    a = jnp.exp(log_a)
    b = jnp.sqrt(-jnp.tanh(log_a) * (a * a + 1.0)) * (ig * u)

    row = lax.broadcasted_iota(jnp.int32, a.shape, 0)
    d = 1
    while d < ts:
        keep = row >= d
        a_s = jnp.where(keep, pltpu.roll(a, d, axis=0), 1.0)
        b_s = jnp.where(keep, pltpu.roll(b, d, axis=0), 0.0)
        b = a * b_s + b
        a = a * a_s
        d *= 2

    @pl.when(first)
    def _():
        h_ref[...] = jnp.zeros(h_ref.shape, F32)

    h = b + a * h_ref[0:1, :]
    h_ref[...] = jnp.broadcast_to(h[ts - 1:ts, :], h_ref.shape)
    o_ref[...] = (h * jax.nn.gelu(gate)).astype(o_ref.dtype)


def _lru(ug, cw, cb, wa, ba, wx, bx, lam, *, batch, ts=256):
    t, w2 = ug.shape
    w = w2 // 2
    ns = t // batch // ts
    return pl.pallas_call(
        _lru_kernel,
        out_shape=jax.ShapeDtypeStruct((t, w), BF16),
        grid=(batch, ns),
        in_specs=[pl.BlockSpec((ts, w2), lambda b, s: (b * ns + s, 0)),
                  _resident(cw.shape), _resident(cb.shape), _resident(wa.shape), _resident(ba.shape),
                  _resident(wx.shape), _resident(bx.shape), _resident(lam.shape)],
        out_specs=pl.BlockSpec((ts, w), lambda b, s: (b * ns + s, 0)),
        scratch_shapes=[pltpu.VMEM((_TAIL + ts, w), F32), pltpu.VMEM((8, w), F32)],
        compiler_params=_cparams("arbitrary", "arbitrary"),
        name="lru",
    )(ug, cw, cb, wa, ba, wx, bx, lam)


def _fox_prep_kernel(sm_ref, bf_ref, fq_ref, fk_ref, carry_ref):
    first = pl.program_id(1) == 0
    ts = sm_ref.shape[0]

    @pl.when(first)
    def _():
        carry_ref[...] = jnp.zeros(carry_ref.shape, F32)

    log_f = jax.nn.log_sigmoid(sm_ref[...] + bf_ref[...])
    tri = jnp.where(_lower_tri(ts), 1.0, 0.0).astype(F32)
    fcum = _dot_exact(tri, log_f) + carry_ref[0:1, :]
    carry_ref[...] = jnp.broadcast_to(fcum[ts - 1:ts, :], carry_ref.shape)
    for h in range(ATT_HEADS):
        fq_ref[h] = jnp.broadcast_to(fcum[:, h:h + 1], (ts, LANES))
    fk_ref[...] = fcum.T[0:8, :]


def _fox_prep(small, bf, *, batch, ts=256):
    t = small.shape[0]
    s = t // batch
    ns = s // ts
    return pl.pallas_call(
        _fox_prep_kernel,
        out_shape=[jax.ShapeDtypeStruct((ATT_HEADS, t, LANES), F32),
                   jax.ShapeDtypeStruct((batch, 8, s), F32)],
        grid=(batch, ns),
        in_specs=[pl.BlockSpec((ts, LANES), lambda b, i: (b * ns + i, 0)), _resident(bf.shape)],
        out_specs=[pl.BlockSpec((ATT_HEADS, ts, LANES), lambda b, i: (0, b * ns + i, 0)),
                   pl.BlockSpec((None, 8, ts), lambda b, i: (b, 0, i))],
        scratch_shapes=[pltpu.VMEM((8, LANES), F32)],
        compiler_params=_cparams("arbitrary", "arbitrary"),
        name="fox_prep",
    )(small, bf)


def _fox_kernel(q_ref, k_ref, ve_ref, vo_ref, fq_ref, fk_ref, o_ref, acc_ref, m_ref, *, tq):
    qi = pl.program_id(2)
    q2 = q_ref[...]
    lane = lax.broadcasted_iota(jnp.int32, (tq, LANES), 1)
    low = lane < HEAD_DIM
    zero = jnp.zeros_like(q2)
    qs = (jnp.where(low, q2, zero), jnp.where(low, zero, q2))
    v_refs = (ve_ref, vo_ref)
    reps = tq // LANES

    def scores(h, ki):
        start = pl.multiple_of(ki * tq, tq)
        kb = k_ref[pl.ds(start, tq), :]
        s = lax.dot_general(qs[h], kb, (((1,), (1,)), ((), ())), preferred_element_type=F32)
        return s + jnp.tile(fq_ref[h], (1, reps)) - fk_ref[h, ki]

    def pv(h, p, ki):
        start = pl.multiple_of(ki * tq, tq)
        return _dot(p.astype(BF16), v_refs[h][pl.ds(start, tq), :])

    causal = _lower_tri(tq)
    for h in range(2):
        s = jnp.where(causal, scores(h, qi), -jnp.inf)
        m = jnp.max(s, axis=1, keepdims=True)
        p = jnp.exp(s - m)
        acc_ref[h] = pv(h, p, qi)
        m_ref[h] = jnp.broadcast_to(m, (tq, LANES))

    def body(ki, carry):
        for h in range(2):
            s = scores(h, ki)
            m_prev = m_ref[h]
            m_new = jnp.maximum(m_prev, jnp.max(s, axis=1, keepdims=True))
            p = jnp.exp(s - jnp.tile(m_new, (1, reps)))
            acc_ref[h] = jnp.exp(m_prev - m_new) * acc_ref[h] + pv(h, p, ki)
            m_ref[h] = m_new
        return carry

    lax.fori_loop(0, qi, body, 0)

    num = jnp.where(low, acc_ref[0], acc_ref[1])
    den = pltpu.roll(jnp.where(low, acc_ref[1], acc_ref[0]), HEAD_DIM, axis=1)
    o_ref[...] = (num / den).astype(o_ref.dtype)


def _fox(q, k, ve, vo, fq, fk, *, batch, tq=512):
    t, aw = q.shape
    s = t // batch
    nq = s // tq
    pairs = aw // LANES
    fk = fk.reshape(batch, ATT_HEADS, nq, 1, tq)
    kv = pl.BlockSpec((s, LANES), lambda b, hp, i: (b, hp))
    return pl.pallas_call(
        functools.partial(_fox_kernel, tq=tq),
        out_shape=jax.ShapeDtypeStruct((t, aw), BF16),
        grid=(batch, pairs, nq),
        in_specs=[pl.BlockSpec((tq, LANES), lambda b, hp, i: (b * nq + i, hp)),
                  kv, kv, kv,
                  pl.BlockSpec((2, tq, LANES), lambda b, hp, i: (hp, b * nq + i, 0)),
                  pl.BlockSpec((None, 2, nq, 1, tq), lambda b, hp, i: (b, hp, 0, 0, 0))],
        out_specs=pl.BlockSpec((tq, LANES), lambda b, hp, i: (b * nq + i, hp)),
        scratch_shapes=[pltpu.VMEM((2, tq, LANES), F32), pltpu.VMEM((2, tq, LANES), F32)],
        compiler_params=_cparams("parallel", "parallel", "arbitrary"),
        name="fox",
    )(q, k, ve, vo, fq, fk)


def _ssd_kernel(xbc_ref, z_ref, sm_ref, cw_ref, cb_ref, dtb_ref, alog_ref, dskip_ref, ng_ref,
                exp_ref, rep_ref, o_ref, buf_ref, state_ref):
    first = pl.program_id(1) == 0
    ts = xbc_ref.shape[0]
    width = z_ref.shape[1]
    gw = width // SSD_GROUPS
    L = SSD_CHUNK

    @pl.when(first)
    def _():
        state_ref[...] = jnp.zeros(state_ref.shape, F32)

    conv = _causal_conv(buf_ref, xbc_ref[...], cw_ref[...], cb_ref[...], first)
    xbc = conv * jax.nn.sigmoid(conv)
    dt = jax.nn.softplus(sm_ref[...] + dtb_ref[...])
    a = -jnp.exp(alog_ref[...]) * dt
    tri = jnp.where(_lower_tri(L), 1.0, 0.0).astype(F32)
    causal = _lower_tri(L)
    lane = lax.broadcasted_iota(jnp.int32, (L, LANES), 1)
    low = lane < HEAD_DIM
    expand = exp_ref[...]
    repl = rep_ref[...]
    d_exp = dskip_ref[...]

    for c in range(ts // L):
        rows = slice(c * L, (c + 1) * L)
        xs = xbc[rows, 0:width]
        bm = xbc[rows, width:width + SSD_GROUPS * SSD_STATE].astype(BF16)
        cm = xbc[rows, width + SSD_GROUPS * SSD_STATE:].astype(BF16)
        a_cum = _dot_exact(tri, a[rows])
        a_cum_t = a_cum.T
        a_exp = _dot_exact(a_cum, expand)
        a_rep = _dot_exact(a_cum, repl)
        xdt = xs * _dot_exact(dt[rows], expand)
        xdt_b = xdt.astype(BF16)
        decay_in = jnp.exp(a_exp)
        a_last = a_exp[L - 1:L, :]
        decay_out = jnp.exp(a_last - a_exp)
        xdec = (decay_out * xdt).astype(BF16)
        chunk_decay = jnp.exp(a_last)
        y_parts = []
        for g in range(SSD_GROUPS):
            bg = bm[:, g * SSD_STATE:(g + 1) * SSD_STATE]
            cg = cm[:, g * SSD_STATE:(g + 1) * SSD_STATE]
            gmat = lax.dot_general(cg, bg, (((1,), (1,)), ((), ())), preferred_element_type=F32)
            for pr in range(gw // LANES):
                col = g * gw + pr * LANES
                hsl = slice(col, col + LANES)
                ys = []
                for e in range(2):
                    hd = col // HEAD_DIM + e
                    seg = a_rep[:, hd * LANES:(hd + 1) * LANES] - a_cum_t[hd + ATT_HEADS:hd + ATT_HEADS + 1, :]
                    lmat = jnp.exp(jnp.where(causal, seg, -jnp.inf))
                    ys.append(_dot((gmat * lmat).astype(BF16), xdt_b[:, hsl]))
                y_diag = jnp.where(low, ys[0], ys[1])
                st = state_ref[:, hsl]
                y_off = _dot(cg, st.astype(BF16)) * decay_in[:, hsl]
                new = lax.dot_general(bg, xdec[:, hsl], (((0,), (0,)), ((), ())),
                                      preferred_element_type=F32)
                state_ref[:, hsl] = chunk_decay[:, hsl] * st + new
                y_parts.append(y_diag + y_off)
        y = jnp.concatenate(y_parts, axis=1) + d_exp * xs
        zc = z_ref[rows, :]
        y = y * (zc * jax.nn.sigmoid(zc))
        outs = []
        for g in range(SSD_GROUPS):
            yg = y[:, g * gw:(g + 1) * gw]
            ms = jnp.mean(yg * yg, axis=-1, keepdims=True)
            outs.append(yg * lax.rsqrt(ms + RMS_EPS))
        o_ref[rows, :] = (jnp.concatenate(outs, axis=1) * ng_ref[...]).astype(o_ref.dtype)


def _ssd(xbc, z, small, cw, cb, dtb, alog, dskip, ng, expand, repl, *, batch, ts=256):
    t, cdim = xbc.shape
    width = z.shape[1]
    ns = t // batch // ts

    def row(n):
        return pl.BlockSpec((ts, n), lambda b, s: (b * ns + s, 0))

    consts = [cw, cb, dtb, alog, dskip, ng, expand, repl]
    return pl.pallas_call(
        _ssd_kernel,
        out_shape=jax.ShapeDtypeStruct((t, width), BF16),
        grid=(batch, ns),
        in_specs=[row(cdim), row(width), row(LANES)] + [_resident(c.shape) for c in consts],
        out_specs=row(width),
        scratch_shapes=[pltpu.VMEM((_TAIL + ts, cdim), F32), pltpu.VMEM((SSD_STATE, width), F32)],
        compiler_params=_cparams("arbitrary", "arbitrary"),
        name="ssd",
    )(xbc, z, small, *consts)


def _out_ln_kernel(x_ref, ya_ref, yb_ref, yc_ref, w_ref, g_ref, b_ref, o_ref, *, alpha):
    na, nb = ya_ref.shape[1], yb_ref.shape[1]
    mix = (_dot(ya_ref[...], w_ref[0:na, :]) + _dot(yb_ref[...], w_ref[na:na + nb, :])
           + _dot(yc_ref[...], w_ref[na + nb:, :]))
    o_ref[...] = _layer_norm(alpha * x_ref[...] + mix, g_ref[...], b_ref[...])


def _out_ln(x, ya, yb, yc, w, g, b, *, alpha, tm=512):
    t, d = x.shape

    def row(n):
        return pl.BlockSpec((tm, n), lambda i: (i, 0))

    return pl.pallas_call(
        functools.partial(_out_ln_kernel, alpha=alpha),
        out_shape=jax.ShapeDtypeStruct((t, d), F32),
        grid=(t // tm,),
        in_specs=[row(d), row(ya.shape[1]), row(yb.shape[1]), row(yc.shape[1]),
                  _resident(w.shape), _resident(g.shape), _resident(b.shape)],
        out_specs=row(d),
        compiler_params=_cparams("parallel"),
        name="out_ln",
    )(x, ya, yb, yc, w, g, b)


def _pe_kernel(x_ref, p_ref, wg_ref, bg_ref, wp_ref, o_ref):
    x = x_ref[...]
    gate = jax.nn.sigmoid(_dot(x.astype(BF16), wg_ref[...]) + bg_ref[...])
    o_ref[...] = x + gate * _dot(p_ref[...].astype(BF16), wp_ref[...])


def _pe(x, p, wg, bg, wp, *, tm=512):
    t, d = x.shape

    def row(n):
        return pl.BlockSpec((tm, n), lambda i: (i, 0))

    return pl.pallas_call(
        _pe_kernel,
        out_shape=jax.ShapeDtypeStruct((t, d), F32),
        grid=(t // tm,),
        in_specs=[row(d), row(p.shape[1]), _resident(wg.shape), _resident(bg.shape), _resident(wp.shape)],
        out_specs=row(d),
        compiler_params=_cparams("parallel"),
        name="pe",
    )(x, p, wg, bg, wp)


def _block_diag(w):
    h, i, j = w.shape
    eye = jnp.eye(h, dtype=w.dtype)
    return (eye[:, None, :, None] * w[:, :, None, :]).reshape(h * i, h * j)


def _row(v):
    return v.reshape(1, -1).astype(F32)


def _head_lanes(v, offset):
    return jnp.zeros((1, LANES), F32).at[0, offset:offset + v.shape[0]].set(v.astype(F32))


def kernel(x, p, ln1_g, ln1_b, ffn1_wg, ffn1_wu, ffn1_wd, w_in, lru_conv_w, lru_conv_b, lru_wa, lru_ba, lru_wx, lru_bx, lru_lambda, fox_bf, ssd_conv_w, ssd_conv_b, ssd_dt_bias, ssd_a_log, ssd_d, ssd_norm_g, w_out, ln2_g, ln2_b, ffn2_wg, ffn2_wu, ffn2_wd, ln3_g, ln3_b, pe_proj, pe_gate_w, pe_gate_b):
    batch, seq, d_model = x.shape
    depth = p.shape[0]
    t = batch * seq
    alpha = (2.0 * depth) ** 0.25
    lru_w = lru_conv_w.shape[-1]
    att_w = ATT_HEADS * HEAD_DIM
    ssd_w = SSD_HEADS * HEAD_DIM
    conv_dim = ssd_conv_w.shape[-1]
    dt_off = ATT_HEADS

    src = jnp.arange(LANES)[:, None]
    expand = (src == dt_off + jnp.arange(ssd_w)[None, :] // HEAD_DIM).astype(F32)
    repl = (src == dt_off + jnp.arange(SSD_HEADS * LANES)[None, :] // LANES).astype(F32)

    xf = x.reshape(t, d_model)
    for i in range(depth):
        sizes = (lru_w, lru_w, att_w, att_w, att_w, ATT_HEADS, ssd_w, conv_dim, SSD_HEADS)
        offs = [0]
        for s_ in sizes:
            offs.append(offs[-1] + s_)
        col = lambda j: w_in[i][:, offs[j]:offs[j + 1]]
        pad = jnp.zeros((d_model, LANES - ATT_HEADS - SSD_HEADS), F32)
        w_cat = jnp.concatenate([col(0), col(1), col(2), col(3), col(4), col(6), col(7),
                                 col(5), col(8), pad], axis=1).astype(BF16)

        xf = _ffn_ln(xf, ffn1_wg[i].astype(BF16), ffn1_wu[i].astype(BF16), ffn1_wd[i].astype(BF16),
                     _row(ln1_g[i]), _row(ln1_b[i]), alpha=alpha)
        ug, q, k, ve, vo, z, xbc, small = _in_proj(xf, w_cat)

        y_a = _lru(ug, lru_conv_w[i], _row(lru_conv_b[i]), _block_diag(lru_wa[i]).astype(BF16),
                   _row(lru_ba[i]), _block_diag(lru_wx[i]).astype(BF16), _row(lru_bx[i]),
                   _row(lru_lambda[i]), batch=batch)

        fq, fk = _fox_prep(small, _head_lanes(fox_bf[i], 0), batch=batch)
        y_b = _fox(q, k, ve, vo, fq, fk[:, 0:ATT_HEADS, :], batch=batch)

        d_exp = jnp.repeat(ssd_d[i].astype(F32), HEAD_DIM).reshape(1, ssd_w)
        y_c = _ssd(xbc, z, small, ssd_conv_w[i], _row(ssd_conv_b[i]),
                   _head_lanes(ssd_dt_bias[i], dt_off), _head_lanes(ssd_a_log[i], dt_off),
                   d_exp, _row(ssd_norm_g[i]), expand, repl, batch=batch)

        xf = _out_ln(xf, y_a, y_b, y_c, w_out[i].astype(BF16), _row(ln2_g[i]), _row(ln2_b[i]), alpha=alpha)
        xf = _ffn_ln(xf, ffn2_wg[i].astype(BF16), ffn2_wu[i].astype(BF16), ffn2_wd[i].astype(BF16),
                     _row(ln3_g[i]), _row(ln3_b[i]), alpha=alpha)
        xf = _pe(xf, p[i].reshape(t, -1), pe_gate_w[i].astype(BF16), _row(pe_gate_b[i]),
                 pe_proj[i].astype(BF16))
    return xf.reshape(batch, seq, d_model)
```

```python
import functools
import math

import numpy as np
import jax
import jax.numpy as jnp
from jax import lax
from jax.experimental import pallas as pl
from jax.experimental.pallas import tpu as pltpu

F32 = jnp.float32
BF16 = jnp.bfloat16

LANES = 128
HEAD_DIM = 64
LRU_C = 8.0
CONV_K = 4
SSD_CHUNK = 128
SSD_STATE = 128
SSD_HEADS = 8
SSD_GROUPS = 2
ATT_HEADS = 4
DT_LANE0 = ATT_HEADS
AUG_STRIDE = 8
LN_EPS = 1e-5
RMS_EPS = 1e-5
LOG2E = math.log2(math.e)
VMEM_LIMIT = 56 * 1024 * 1024


def _cparams(*sem):
    return pltpu.CompilerParams(dimension_semantics=sem, vmem_limit_bytes=VMEM_LIMIT)


def _resident(shape):
    nd = len(shape)
    return pl.BlockSpec(shape, lambda *_: (0,) * nd, pipeline_mode=pl.Buffered(1))


def _layer_norm(y, g, b):
    mu = jnp.mean(y, axis=-1, keepdims=True)
    d = y - mu
    var = jnp.mean(d * d, axis=-1, keepdims=True)
    return d * lax.rsqrt(var + LN_EPS) * g + b


def _dot(a, b):
    return jnp.dot(a, b, preferred_element_type=F32)


def _split3(x):
    hi = x.astype(BF16)
    r = x - hi.astype(F32)
    mid = r.astype(BF16)
    lo = (r - mid.astype(F32)).astype(BF16)
    return hi, mid, lo


def _cumsum_rows(tri3, x):
    return _dot(tri3, jnp.concatenate(_split3(x), axis=0))


def _place_lanes(x, w3):
    return _dot(jnp.concatenate(_split3(x), axis=1), w3)


def _ffn_ln_kernel(x_ref, wg_ref, wu_ref, wd_ref, g_ref, b_ref, o_ref, *, alpha, fchunk):
    x = x_ref[...]
    xb = x.astype(BF16)
    ffn = wg_ref.shape[1]
    acc = jnp.zeros(x.shape, F32)
    for c0 in range(0, ffn, fchunk):
        c1 = min(c0 + fchunk, ffn)
        g = _dot(xb, wg_ref[:, c0:c1])
        u = _dot(xb, wu_ref[:, c0:c1])
        h = (g * jax.nn.sigmoid(g) * u).astype(BF16)
        acc = acc + _dot(h, wd_ref[c0:c1, :])
    o_ref[...] = _layer_norm(alpha * x + 0.5 * acc, g_ref[...], b_ref[...])


def _ffn_ln(x, wg, wu, wd, g, b, *, alpha, tm=512, fchunk=512):
    t, d = x.shape
    row = pl.BlockSpec((tm, d), lambda i: (i, 0))
    return pl.pallas_call(
        functools.partial(_ffn_ln_kernel, alpha=alpha, fchunk=fchunk),
        out_shape=jax.ShapeDtypeStruct((t, d), F32),
        grid=(t // tm,),
        in_specs=[row, _resident(wg.shape), _resident(wu.shape), _resident(wd.shape),
                  _resident(g.shape), _resident(b.shape)],
        out_specs=row,
        compiler_params=_cparams("parallel"),
        name="ffn_ln",
    )(x, wg, wu, wd, g, b)


_IN_UG, _IN_QKV, _IN_Z, _IN_XBC, _IN_SMALL = 512, 768, 512, 1024, LANES


def _in_proj_kernel(x_ref, w_ref, ug_ref, q_ref, k_ref, ve_ref, vo_ref, z_ref, xbc_ref, sm_ref):
    h = _dot(x_ref[...].astype(BF16), w_ref[...])
    o = 0
    ug_ref[...] = h[:, o:o + _IN_UG]
    o += _IN_UG
    aw = _IN_QKV // 3
    q = h[:, o:o + aw]
    k = h[:, o + aw:o + 2 * aw]
    v = h[:, o + 2 * aw:o + 3 * aw]
    o += _IN_QKV
    q_ref[...] = (q * (HEAD_DIM ** -0.5 * LOG2E)).astype(BF16)
    k_ref[...] = k.astype(BF16)
    even = (lax.broadcasted_iota(jnp.int32, v.shape, 1) % LANES) < HEAD_DIM
    ve_ref[...] = jnp.where(even, v, 1.0).astype(BF16)
    vo_ref[...] = jnp.where(even, 1.0, v).astype(BF16)
    z_ref[...] = h[:, o:o + _IN_Z]
    o += _IN_Z
    xbc_ref[...] = h[:, o:o + _IN_XBC]
    o += _IN_XBC
    sm_ref[...] = h[:, o:o + _IN_SMALL]


def _in_proj(x, w, *, tm=512):
    t, d = x.shape
    aw = _IN_QKV // 3

    def row(n):
        return pl.BlockSpec((tm, n), lambda i: (i, 0))

    outs = [(_IN_UG, F32), (aw, BF16), (aw, BF16), (aw, BF16), (aw, BF16),
            (_IN_Z, F32), (_IN_XBC, F32), (_IN_SMALL, F32)]
    return pl.pallas_call(
        _in_proj_kernel,
        out_shape=[jax.ShapeDtypeStruct((t, n), dt) for n, dt in outs],
        grid=(t // tm,),
        in_specs=[row(d), _resident(w.shape)],
        out_specs=[row(n) for n, _ in outs],
        compiler_params=_cparams("parallel"),
        name="in_proj",
    )(x, w)


_TAIL = 8


def _causal_conv(buf_ref, x, w, b, first):
    ts = x.shape[0]

    @pl.when(first)
    def _():
        buf_ref[0:_TAIL, :] = jnp.zeros((_TAIL, x.shape[1]), F32)

    buf_ref[_TAIL:_TAIL + ts, :] = x
    ext = buf_ref[...]
    y = b + w[CONV_K - 1:CONV_K, :] * x
    for j in range(CONV_K - 1):
        back = CONV_K - 1 - j
        y = y + w[j:j + 1, :] * pltpu.roll(ext, back, axis=0)[_TAIL:_TAIL + ts, :]
    buf_ref[0:_TAIL, :] = x[ts - _TAIL:ts, :]
    return y


def _tri3(n):
    tri = np.tril(np.ones((n, n), np.float32))
    return jnp.asarray(np.concatenate([tri, tri, tri], axis=1), BF16)


def _neg_mask(n):
    return jnp.asarray(np.where(np.tril(np.ones((n, n), bool)), 0.0, -np.inf), F32)


def _lru_kernel(ug_ref, cw_ref, cb_ref, wa_ref, ba_ref, wx_ref, bx_ref, lam_ref, o_ref,
                buf_ref, h_ref):
    first = pl.program_id(1) == 0
    w = ug_ref.shape[1] // 2
    ts = ug_ref.shape[0]
    u_raw = ug_ref[:, 0:w]
    gate = ug_ref[:, w:2 * w]
    u = _causal_conv(buf_ref, u_raw, cw_ref[...], cb_ref[...], first)
    ub = u.astype(BF16)
    r = jax.nn.sigmoid(_dot(ub, wa_ref[...]) + ba_ref[...])
    ig = jax.nn.sigmoid(_dot(ub, wx_ref[...]) + bx_ref[...])
    log_a = -LRU_C * r * jax.nn.softplus(-lam_ref[...])
    a = jnp.exp(log_a)
    b = jnp.sqrt(-jnp.tanh(log_a) * (a * a + 1.0)) * (ig * u)

    row = lax.broadcasted_iota(jnp.int32, a.shape, 0)
    d = 1
    while d < ts:
        keep = row >= d
        a_s = jnp.where(keep, pltpu.roll(a, d, axis=0), 1.0)
        b_s = jnp.where(keep, pltpu.roll(b, d, axis=0), 0.0)
        b = a * b_s + b
        a = a * a_s
        d *= 2

    @pl.when(first)
    def _():
        h_ref[...] = jnp.zeros(h_ref.shape, F32)

    h = b + a * h_ref[0:1, :]
    h_ref[...] = jnp.broadcast_to(h[ts - 1:ts, :], h_ref.shape)
    o_ref[...] = (h * jax.nn.gelu(gate)).astype(o_ref.dtype)


def _lru(ug, cw, cb, wa, ba, wx, bx, lam, *, batch, ts=256):
    t, w2 = ug.shape
    w = w2 // 2
    ns = t // batch // ts
    return pl.pallas_call(
        _lru_kernel,
        out_shape=jax.ShapeDtypeStruct((t, w), BF16),
        grid=(batch, ns),
        in_specs=[pl.BlockSpec((ts, w2), lambda b, s: (b * ns + s, 0)),
                  _resident(cw.shape), _resident(cb.shape), _resident(wa.shape), _resident(ba.shape),
                  _resident(wx.shape), _resident(bx.shape), _resident(lam.shape)],
        out_specs=pl.BlockSpec((ts, w), lambda b, s: (b * ns + s, 0)),
        scratch_shapes=[pltpu.VMEM((_TAIL + ts, w), F32), pltpu.VMEM((8, w), F32)],
        compiler_params=_cparams("arbitrary", "arbitrary"),
        name="lru",
    )(ug, cw, cb, wa, ba, wx, bx, lam)


def _fox_aug_consts():
    pairs = ATT_HEADS // 2
    place = np.zeros((3, LANES, 2 * pairs * LANES), np.float32)
    bias = np.zeros((1, 2 * pairs * LANES), np.float32)
    for h in range(ATT_HEADS):
        pr, e = divmod(h, 2)
        qcol = pr * LANES + AUG_STRIDE * e
        kcol = (pairs + pr) * LANES + AUG_STRIDE * e
        for term in range(3):
            place[term, h, qcol + term] = 1.0
            place[term, h, kcol + 3 + term] = -1.0
        bias[0, qcol + 3:qcol + 6] = 1.0
        bias[0, kcol:kcol + 3] = 1.0
    return jnp.asarray(place.reshape(3 * LANES, -1), BF16), jnp.asarray(bias, F32)


def _fox_prep_kernel(sm_ref, bf_ref, tri3_ref, place_ref, bias_ref, aq_ref, ak_ref, carry_ref, *, sub):
    first = pl.program_id(1) == 0
    ts = sm_ref.shape[0]
    pairs = aq_ref.shape[0]

    @pl.when(first)
    def _():
        carry_ref[...] = jnp.zeros(carry_ref.shape, F32)

    for r in range(ts // sub):
        rows = slice(r * sub, (r + 1) * sub)
        log_f = jax.nn.log_sigmoid(sm_ref[rows, :] + bf_ref[...])
        fcum = _cumsum_rows(tri3_ref[...], log_f) + carry_ref[0:1, :]
        carry_ref[...] = jnp.broadcast_to(fcum[sub - 1:sub, :], carry_ref.shape)
        aug = (_place_lanes(fcum * LOG2E, place_ref[...]) + bias_ref[...]).astype(BF16)
        for pr in range(pairs):
            aq_ref[pr, rows, :] = aug[:, pr * LANES:(pr + 1) * LANES]
            ak_ref[pr, rows, :] = aug[:, (pairs + pr) * LANES:(pairs + pr + 1) * LANES]


def _fox_prep(small, bf, *, batch, ts=1024, sub=256):
    t = small.shape[0]
    ns = t // batch // ts
    pairs = ATT_HEADS // 2
    place, bias = _fox_aug_consts()
    tri3 = _tri3(sub)
    out = pl.BlockSpec((pairs, ts, LANES), lambda b, i: (0, b * ns + i, 0))
    return pl.pallas_call(
        functools.partial(_fox_prep_kernel, sub=sub),
        out_shape=[jax.ShapeDtypeStruct((pairs, t, LANES), BF16)] * 2,
        grid=(batch, ns),
        in_specs=[pl.BlockSpec((ts, LANES), lambda b, i: (b * ns + i, 0)), _resident(bf.shape),
                  _resident(tri3.shape), _resident(place.shape), _resident(bias.shape)],
        out_specs=[out, out],
        scratch_shapes=[pltpu.VMEM((8, LANES), F32)],
        compiler_params=_cparams("arbitrary", "arbitrary"),
        name="fox_prep",
    )(small, bf, tri3, place, bias)


def _fox_kernel(q_ref, aq_ref, k_ref, ak_ref, ve_ref, vo_ref, nm_ref, o_ref,
                acc_ref, m_ref, s_ref, p_ref, al_ref, mx_ref, *, tq):
    tk = tq
    qi = pl.program_id(2)
    q2 = q_ref[...]
    aq = aq_ref[...]
    lane = lax.broadcasted_iota(jnp.int32, (tq, LANES), 1)
    low = lane < HEAD_DIM
    zero = jnp.zeros_like(q2)
    qs = []
    for e in range(2):
        own = jnp.where(low, q2, zero) if e == 0 else jnp.where(low, zero, q2)
        aug = jnp.where((lane >= AUG_STRIDE * e) & (lane < AUG_STRIDE * (e + 1)), aq, zero)
        qs.append(jnp.concatenate([own, aug], axis=1))
    v_refs = (ve_ref, vo_ref)
    reps = tk // LANES

    def rows_of(n):
        return pl.ds(pl.multiple_of(n * tk, tk), tk)

    def keys(n):
        return jnp.concatenate([k_ref[rows_of(n), :], ak_ref[rows_of(n), :]], axis=1)

    def scores(h, kb):
        return lax.dot_general(qs[h], kb, (((1,), (1,)), ((), ())), preferred_element_type=F32)

    def accumulate(h, slot, n):
        acc_ref[h] = al_ref[slot, h] * acc_ref[h] + _dot(p_ref[slot, h], v_refs[h][rows_of(n), :])

    def produce(slot, h, kb):
        s = scores(h, kb)
        s_ref[slot, h] = s
        mx_ref[slot, h] = jnp.broadcast_to(jnp.max(s, axis=1, keepdims=True), (tq, LANES))

    def stage(cur, n, mask_ref=None, lookahead=True):
        prv = 1 - cur
        if lookahead:
            kb = keys(n + 1)
        for h in range(2):
            s = s_ref[cur, h]
            m_prev = m_ref[h]
            if mask_ref is None:
                m_new = jnp.maximum(m_prev, mx_ref[cur, h])
            else:
                s = s + mask_ref[...]
                m_new = jnp.maximum(m_prev, jnp.max(s, axis=1, keepdims=True))
            al_ref[cur, h] = jnp.exp2(m_prev - m_new)
            m_ref[h] = m_new
            p_ref[cur, h] = jnp.exp2(s - jnp.tile(m_new, (1, reps))).astype(BF16)
            if lookahead:
                produce(prv, h, kb)
            accumulate(h, prv, jnp.maximum(n - 1, 0))

    acc_ref[...] = jnp.zeros(acc_ref.shape, F32)
    m_ref[...] = jnp.full(m_ref.shape, -jnp.inf, F32)
    p_ref[1] = jnp.zeros(p_ref.shape[1:], BF16)
    al_ref[1] = jnp.zeros(al_ref.shape[1:], F32)
    kb = keys(0)
    for h in range(2):
        produce(0, h, kb)

    def body(j, carry):
        stage(0, 2 * j)
        stage(1, 2 * j + 1)
        return carry

    lax.fori_loop(0, qi // 2, body, 0)

    @pl.when(qi % 2 == 0)
    def _():
        stage(0, qi, mask_ref=nm_ref, lookahead=False)
        for h in range(2):
            accumulate(h, 0, qi)

    @pl.when(qi % 2 == 1)
    def _():
        stage(0, qi - 1)
        stage(1, qi, mask_ref=nm_ref, lookahead=False)
        for h in range(2):
            accumulate(h, 1, qi)

    num = jnp.where(low, acc_ref[0], acc_ref[1])
    den = pltpu.roll(jnp.where(low, acc_ref[1], acc_ref[0]), HEAD_DIM, axis=1)
    o_ref[...] = (num / den).astype(o_ref.dtype)


def _fox(q, k, ve, vo, aq, ak, *, batch, tq=512):
    t, aw = q.shape
    s = t // batch
    nq = s // tq
    pairs = aw // LANES
    nm = _neg_mask(tq)
    qblk = pl.BlockSpec((tq, LANES), lambda b, hp, i: (b * nq + i, hp))
    kv = pl.BlockSpec((s, LANES), lambda b, hp, i: (b, hp))
    return pl.pallas_call(
        functools.partial(_fox_kernel, tq=tq),
        out_shape=jax.ShapeDtypeStruct((t, aw), BF16),
        grid=(batch, pairs, nq),
        in_specs=[qblk,
                  pl.BlockSpec((None, tq, LANES), lambda b, hp, i: (hp, b * nq + i, 0)),
                  kv,
                  pl.BlockSpec((None, s, LANES), lambda b, hp, i: (hp, b, 0)),
                  kv, kv, _resident(nm.shape)],
        out_specs=qblk,
        scratch_shapes=[pltpu.VMEM((2, tq, LANES), F32), pltpu.VMEM((2, tq, LANES), F32),
                        pltpu.VMEM((2, 2, tq, tq), F32), pltpu.VMEM((2, 2, tq, tq), BF16),
                        pltpu.VMEM((2, 2, tq, LANES), F32), pltpu.VMEM((2, 2, tq, LANES), F32)],
        compiler_params=_cparams("parallel", "parallel", "arbitrary"),
        name="fox",
    )(q, aq, k, ak, ve, vo, nm)


def _ssd_kernel(xbc_ref, z_ref, sm_ref, cw_ref, cb_ref, dtb_ref, alog_ref, dskip_ref, ng_ref,
                tri3_ref, nm_ref, exp3_ref, o_ref, buf_ref, xs_ref, dt_ref, a_ref, state_ref):
    first = pl.program_id(1) == 0
    ts = xbc_ref.shape[0]
    width = z_ref.shape[1]
    gw = width // SSD_GROUPS
    L = SSD_CHUNK
    nst = SSD_GROUPS * SSD_STATE

    @pl.when(first)
    def _():
        state_ref[...] = jnp.zeros(state_ref.shape, F32)

    conv = _causal_conv(buf_ref, xbc_ref[...], cw_ref[...], cb_ref[...], first)
    xs_ref[...] = conv * jax.nn.sigmoid(conv)
    dt = jax.nn.softplus(sm_ref[...] + dtb_ref[...])
    dt_ref[...] = dt
    a_ref[...] = -jnp.exp(alog_ref[...]) * dt

    low = lax.broadcasted_iota(jnp.int32, (L, LANES), 1) < HEAD_DIM

    def chunk(c, carry):
        rows = pl.ds(pl.multiple_of(c * L, L), L)
        xs = xs_ref[rows, 0:width]
        bm = xs_ref[rows, width:width + nst].astype(BF16)
        cm = xs_ref[rows, width + nst:width + 2 * nst].astype(BF16)
        a_cum = _cumsum_rows(tri3_ref[...], a_ref[rows, :])
        a_cum_t = a_cum.T
        a_exp = _place_lanes(a_cum, exp3_ref[...])
        xdt = xs * _place_lanes(dt_ref[rows, :], exp3_ref[...])
        decay_in = jnp.exp(a_exp)
        a_last = a_exp[L - 1:L, :]
        xdec = (jnp.exp(a_last - a_exp) * xdt).astype(BF16)
        chunk_decay = jnp.exp(a_last)
        zero = jnp.zeros((L, LANES), F32)
        y_parts = []
        for g in range(SSD_GROUPS):
            bg = bm[:, g * SSD_STATE:(g + 1) * SSD_STATE]
            cg = cm[:, g * SSD_STATE:(g + 1) * SSD_STATE]
            gmat = lax.dot_general(cg, bg, (((1,), (1,)), ((), ())), preferred_element_type=F32)
            for pr in range(gw // LANES):
                col = g * gw + pr * LANES
                hsl = slice(col, col + LANES)
                hd = col // HEAD_DIM
                blk = a_exp[:, hsl]
                swp = pltpu.roll(blk, HEAD_DIM, axis=1)
                cols = (jnp.where(low, blk, swp), jnp.where(low, swp, blk))
                ms = []
                for e in range(2):
                    r = DT_LANE0 + hd + e
                    seg = cols[e] - a_cum_t[r:r + 1, :]
                    ms.append((gmat * jnp.exp(seg + nm_ref[...])).astype(BF16))
                xp = xdt[:, hsl]
                rhs = jnp.concatenate([jnp.where(low, xp, zero).astype(BF16),
                                       jnp.where(low, zero, xp).astype(BF16)], axis=0)
                y_diag = _dot(jnp.concatenate(ms, axis=1), rhs)
                st = state_ref[:, hsl]
                y_off = _dot(cg, st.astype(BF16)) * decay_in[:, hsl]
                new = lax.dot_general(bg, xdec[:, hsl], (((0,), (0,)), ((), ())),
                                      preferred_element_type=F32)
                state_ref[:, hsl] = chunk_decay[:, hsl] * st + new
                y_parts.append(y_diag + y_off)
        y = jnp.concatenate(y_parts, axis=1) + dskip_ref[...] * xs
        zc = z_ref[rows, :]
        y = y * (zc * jax.nn.sigmoid(zc))
        outs = []
        for g in range(SSD_GROUPS):
            yg = y[:, g * gw:(g + 1) * gw]
            ms_ = jnp.mean(yg * yg, axis=-1, keepdims=True)
            outs.append(yg * lax.rsqrt(ms_ + RMS_EPS))
        o_ref[rows, :] = (jnp.concatenate(outs, axis=1) * ng_ref[...]).astype(o_ref.dtype)
        return carry

    lax.fori_loop(0, ts // L, chunk, 0)


def _ssd(xbc, z, small, cw, cb, dtb, alog, dskip, ng, *, batch, ts=1024):
    t, cdim = xbc.shape
    width = z.shape[1]
    ns = t // batch // ts
    tri3 = _tri3(SSD_CHUNK)
    nm = _neg_mask(SSD_CHUNK)
    src = np.arange(LANES)[:, None]
    expand = (src == DT_LANE0 + np.arange(width)[None, :] // HEAD_DIM).astype(np.float32)
    exp3 = jnp.asarray(np.concatenate([expand] * 3, axis=0), BF16)

    def row(n):
        return pl.BlockSpec((ts, n), lambda b, s: (b * ns + s, 0))

    consts = [cw, cb, dtb, alog, dskip, ng, tri3, nm, exp3]
    return pl.pallas_call(
        _ssd_kernel,
        out_shape=jax.ShapeDtypeStruct((t, width), BF16),
        grid=(batch, ns),
        in_specs=[row(cdim), row(width), row(LANES)] + [_resident(c.shape) for c in consts],
        out_specs=row(width),
        scratch_shapes=[pltpu.VMEM((_TAIL + ts, cdim), F32), pltpu.VMEM((ts, cdim), F32),
                        pltpu.VMEM((ts, LANES), F32), pltpu.VMEM((ts, LANES), F32),
                        pltpu.VMEM((SSD_STATE, width), F32)],
        compiler_params=_cparams("arbitrary", "arbitrary"),
        name="ssd",
    )(xbc, z, small, *consts)


def _out_ln_kernel(x_ref, ya_ref, yb_ref, yc_ref, w_ref, g_ref, b_ref, o_ref, *, alpha):
    na, nb = ya_ref.shape[1], yb_ref.shape[1]
    mix = (_dot(ya_ref[...], w_ref[0:na, :]) + _dot(yb_ref[...], w_ref[na:na + nb, :])
           + _dot(yc_ref[...], w_ref[na + nb:, :]))
    o_ref[...] = _layer_norm(alpha * x_ref[...] + mix, g_ref[...], b_ref[...])


def _out_ln(x, ya, yb, yc, w, g, b, *, alpha, tm=512):
    t, d = x.shape

    def row(n):
        return pl.BlockSpec((tm, n), lambda i: (i, 0))

    return pl.pallas_call(
        functools.partial(_out_ln_kernel, alpha=alpha),
        out_shape=jax.ShapeDtypeStruct((t, d), F32),
        grid=(t // tm,),
        in_specs=[row(d), row(ya.shape[1]), row(yb.shape[1]), row(yc.shape[1]),
                  _resident(w.shape), _resident(g.shape), _resident(b.shape)],
        out_specs=row(d),
        compiler_params=_cparams("parallel"),
        name="out_ln",
    )(x, ya, yb, yc, w, g, b)


def _pe_kernel(x_ref, p_ref, wg_ref, bg_ref, wp_ref, o_ref):
    x = x_ref[...]
    gate = jax.nn.sigmoid(_dot(x.astype(BF16), wg_ref[...]) + bg_ref[...])
    o_ref[...] = x + gate * _dot(p_ref[...].astype(BF16), wp_ref[...])


def _pe(x, p, wg, bg, wp, *, tm=512):
    t, d = x.shape

    def row(n):
        return pl.BlockSpec((tm, n), lambda i: (i, 0))

    return pl.pallas_call(
        _pe_kernel,
        out_shape=jax.ShapeDtypeStruct((t, d), F32),
        grid=(t // tm,),
        in_specs=[row(d), row(p.shape[1]), _resident(wg.shape), _resident(bg.shape), _resident(wp.shape)],
        out_specs=row(d),
        compiler_params=_cparams("parallel"),
        name="pe",
    )(x, p, wg, bg, wp)


def _block_diag(w):
    h, i, j = w.shape
    eye = jnp.eye(h, dtype=w.dtype)
    return (eye[:, None, :, None] * w[:, :, None, :]).reshape(h * i, h * j)


def _row(v):
    return v.reshape(1, -1).astype(F32)


def _head_lanes(v, offset):
    return jnp.zeros((1, LANES), F32).at[0, offset:offset + v.shape[0]].set(v.astype(F32))


def kernel(x, p, ln1_g, ln1_b, ffn1_wg, ffn1_wu, ffn1_wd, w_in, lru_conv_w, lru_conv_b, lru_wa, lru_ba, lru_wx, lru_bx, lru_lambda, fox_bf, ssd_conv_w, ssd_conv_b, ssd_dt_bias, ssd_a_log, ssd_d, ssd_norm_g, w_out, ln2_g, ln2_b, ffn2_wg, ffn2_wu, ffn2_wd, ln3_g, ln3_b, pe_proj, pe_gate_w, pe_gate_b):
    batch, seq, d_model = x.shape
    depth = p.shape[0]
    t = batch * seq
    alpha = (2.0 * depth) ** 0.25
    lru_w = lru_conv_w.shape[-1]
    att_w = ATT_HEADS * HEAD_DIM
    ssd_w = SSD_HEADS * HEAD_DIM
    conv_dim = ssd_conv_w.shape[-1]

    xf = x.reshape(t, d_model)
    for i in range(depth):
        sizes = (lru_w, lru_w, att_w, att_w, att_w, ATT_HEADS, ssd_w, conv_dim, SSD_HEADS)
        offs = [0]
        for s_ in sizes:
            offs.append(offs[-1] + s_)
        col = lambda j: w_in[i][:, offs[j]:offs[j + 1]]
        pad = jnp.zeros((d_model, LANES - ATT_HEADS - SSD_HEADS), F32)
        w_cat = jnp.concatenate([col(0), col(1), col(2), col(3), col(4), col(6), col(7),
                                 col(5), col(8), pad], axis=1).astype(BF16)

        xf = _ffn_ln(xf, ffn1_wg[i].astype(BF16), ffn1_wu[i].astype(BF16), ffn1_wd[i].astype(BF16),
                     _row(ln1_g[i]), _row(ln1_b[i]), alpha=alpha)
        ug, q, k, ve, vo, z, xbc, small = _in_proj(xf, w_cat)

        y_a = _lru(ug, lru_conv_w[i], _row(lru_conv_b[i]), _block_diag(lru_wa[i]).astype(BF16),
                   _row(lru_ba[i]), _block_diag(lru_wx[i]).astype(BF16), _row(lru_bx[i]),
                   _row(lru_lambda[i]), batch=batch)

        aq, ak = _fox_prep(small, _head_lanes(fox_bf[i], 0), batch=batch)
        y_b = _fox(q, k, ve, vo, aq, ak, batch=batch)

        d_exp = jnp.repeat(ssd_d[i].astype(F32), HEAD_DIM).reshape(1, ssd_w)
        y_c = _ssd(xbc, z, small, ssd_conv_w[i], _row(ssd_conv_b[i]),
                   _head_lanes(ssd_dt_bias[i], DT_LANE0), _head_lanes(ssd_a_log[i], DT_LANE0),
                   d_exp, _row(ssd_norm_g[i]), batch=batch)

        xf = _out_ln(xf, y_a, y_b, y_c, w_out[i].astype(BF16), _row(ln2_g[i]), _row(ln2_b[i]), alpha=alpha)
        xf = _ffn_ln(xf, ffn2_wg[i].astype(BF16), ffn2_wu[i].astype(BF16), ffn2_wd[i].astype(BF16),
                     _row(ln3_g[i]), _row(ln3_b[i]), alpha=alpha)
        xf = _pe(xf, p[i].reshape(t, -1), pe_gate_w[i].astype(BF16), _row(pe_gate_b[i]),
                 pe_proj[i].astype(BF16))
    return xf.reshape(batch, seq, d_model)
```

```python
import functools
import math

import numpy as np
import jax
import jax.numpy as jnp
from jax import lax
from jax.experimental import pallas as pl
from jax.experimental.pallas import tpu as pltpu

F32 = jnp.float32
BF16 = jnp.bfloat16

LANES = 128
HEAD_DIM = 64
LRU_C = 8.0
CONV_K = 4
SSD_CHUNK = 128
SSD_STATE = 128
SSD_HEADS = 8
SSD_GROUPS = 2
ATT_HEADS = 4
DT_LANE0 = ATT_HEADS
AUG_STRIDE = 8
LN_EPS = 1e-5
RMS_EPS = 1e-5
LOG2E = math.log2(math.e)
VMEM_LIMIT = 56 * 1024 * 1024


def _cparams(*sem):
    return pltpu.CompilerParams(dimension_semantics=sem, vmem_limit_bytes=VMEM_LIMIT)


def _resident(shape):
    nd = len(shape)
    return pl.BlockSpec(shape, lambda *_: (0,) * nd, pipeline_mode=pl.Buffered(1))


def _param(arr, layer):
    nd = arr.ndim
    return pl.BlockSpec((None,) + arr.shape[1:], lambda *_: (layer,) + (0,) * (nd - 1),
                        pipeline_mode=pl.Buffered(1))


def _layer_norm(y, g, b):
    mu = jnp.mean(y, axis=-1, keepdims=True)
    d = y - mu
    var = jnp.mean(d * d, axis=-1, keepdims=True)
    return d * lax.rsqrt(var + LN_EPS) * g + b


def _dot(a, b):
    return jnp.dot(a, b, preferred_element_type=F32)


def _split3(x):
    hi = x.astype(BF16)
    r = x - hi.astype(F32)
    mid = r.astype(BF16)
    lo = (r - mid.astype(F32)).astype(BF16)
    return hi, mid, lo


def _cumsum_rows(tri3, x):
    return _dot(tri3, jnp.concatenate(_split3(x), axis=0))


def _place_lanes(x, w3):
    return _dot(jnp.concatenate(_split3(x), axis=1), w3)


FFN_CHUNK = 512


def _ffn_ln_tile(x, wg_ref, wu_ref, wd_ref, g_ref, b_ref, alpha):
    xb = x.astype(BF16)
    ffn = wg_ref.shape[1]
    acc = jnp.zeros(x.shape, F32)
    for c0 in range(0, ffn, FFN_CHUNK):
        c1 = min(c0 + FFN_CHUNK, ffn)
        g = _dot(xb, wg_ref[:, c0:c1])
        u = _dot(xb, wu_ref[:, c0:c1])
        h = (g * jax.nn.sigmoid(g) * u).astype(BF16)
        acc = acc + _dot(h, wd_ref[c0:c1, :])
    return _layer_norm(alpha * x + 0.5 * acc, g_ref[...], b_ref[...])


_IN_UG, _IN_QKV, _IN_Z, _IN_XBC, _IN_SMALL = 512, 768, 512, 1024, LANES


def _pre_kernel(x_ref, wg_ref, wu_ref, wd_ref, g_ref, b_ref, w_ref,
                xo_ref, ug_ref, q_ref, k_ref, ve_ref, vo_ref, z_ref, xbc_ref, sm_ref, *, alpha):
    x1 = _ffn_ln_tile(x_ref[...], wg_ref, wu_ref, wd_ref, g_ref, b_ref, alpha)
    xo_ref[...] = x1
    h = _dot(x1.astype(BF16), w_ref[...])
    o = 0
    ug_ref[...] = h[:, o:o + _IN_UG]
    o += _IN_UG
    aw = _IN_QKV // 3
    q = h[:, o:o + aw]
    k = h[:, o + aw:o + 2 * aw]
    v = h[:, o + 2 * aw:o + 3 * aw]
    o += _IN_QKV
    q_ref[...] = (q * (HEAD_DIM ** -0.5 * LOG2E)).astype(BF16)
    k_ref[...] = k.astype(BF16)
    even = (lax.broadcasted_iota(jnp.int32, v.shape, 1) % LANES) < HEAD_DIM
    ve_ref[...] = jnp.where(even, v, 1.0).astype(BF16)
    vo_ref[...] = jnp.where(even, 1.0, v).astype(BF16)
    z_ref[...] = h[:, o:o + _IN_Z]
    o += _IN_Z
    xbc_ref[...] = h[:, o:o + _IN_XBC]
    o += _IN_XBC
    sm_ref[...] = h[:, o:o + _IN_SMALL]


def _pre(x, params, *, layer, alpha, tm=512):
    t, d = x.shape
    aw = _IN_QKV // 3

    def row(n):
        return pl.BlockSpec((tm, n), lambda i: (i, 0))

    outs = [(d, F32), (_IN_UG, F32), (aw, BF16), (aw, BF16), (aw, BF16), (aw, BF16),
            (_IN_Z, F32), (_IN_XBC, F32), (_IN_SMALL, F32)]
    return pl.pallas_call(
        functools.partial(_pre_kernel, alpha=alpha),
        out_shape=[jax.ShapeDtypeStruct((t, n), dt) for n, dt in outs],
        grid=(t // tm,),
        in_specs=[row(d)] + [_param(c, layer) for c in params],
        out_specs=[row(n) for n, _ in outs],
        compiler_params=_cparams("parallel"),
        name="pre",
    )(x, *params)


_TAIL = 8


def _causal_conv(buf_ref, x, w, b, first):
    ts = x.shape[0]

    @pl.when(first)
    def _():
        buf_ref[0:_TAIL, :] = jnp.zeros((_TAIL, x.shape[1]), F32)

    buf_ref[_TAIL:_TAIL + ts, :] = x
    ext = buf_ref[...]
    y = b + w[CONV_K - 1:CONV_K, :] * x
    for j in range(CONV_K - 1):
        back = CONV_K - 1 - j
        y = y + w[j:j + 1, :] * pltpu.roll(ext, back, axis=0)[_TAIL:_TAIL + ts, :]
    buf_ref[0:_TAIL, :] = x[ts - _TAIL:ts, :]
    return y


def _tri3(n):
    tri = np.tril(np.ones((n, n), np.float32))
    return jnp.asarray(np.concatenate([tri, tri, tri], axis=1), BF16)


def _neg_mask(n):
    return jnp.asarray(np.where(np.tril(np.ones((n, n), bool)), 0.0, -np.inf), F32)


def _lru_kernel(ug_ref, cw_ref, cb_ref, wa_ref, ba_ref, wx_ref, bx_ref, lam_ref, o_ref,
                buf_ref, h_ref):
    first = pl.program_id(1) == 0
    w = ug_ref.shape[1] // 2
    ts = ug_ref.shape[0]
    u_raw = ug_ref[:, 0:w]
    gate = ug_ref[:, w:2 * w]
    u = _causal_conv(buf_ref, u_raw, cw_ref[...], cb_ref[...], first)
    ub = u.astype(BF16)
    r = jax.nn.sigmoid(_dot(ub, wa_ref[...]) + ba_ref[...])
    ig = jax.nn.sigmoid(_dot(ub, wx_ref[...]) + bx_ref[...])
    log_a = -LRU_C * r * jax.nn.softplus(-lam_ref[...])
    a = jnp.exp(log_a)
    b = jnp.sqrt(-jnp.tanh(log_a) * (a * a + 1.0)) * (ig * u)

    row = lax.broadcasted_iota(jnp.int32, a.shape, 0)
    d = 1
    while d < ts:
        keep = row >= d
        a_s = jnp.where(keep, pltpu.roll(a, d, axis=0), 1.0)
        b_s = jnp.where(keep, pltpu.roll(b, d, axis=0), 0.0)
        b = a * b_s + b
        a = a * a_s
        d *= 2

    @pl.when(first)
    def _():
        h_ref[...] = jnp.zeros(h_ref.shape, F32)

    h = b + a * h_ref[0:1, :]
    h_ref[...] = jnp.broadcast_to(h[ts - 1:ts, :], h_ref.shape)
    o_ref[...] = (h * jax.nn.gelu(gate)).astype(o_ref.dtype)


def _lru(ug, params, *, layer, batch, ts=256):
    t, w2 = ug.shape
    w = w2 // 2
    ns = t // batch // ts
    return pl.pallas_call(
        _lru_kernel,
        out_shape=jax.ShapeDtypeStruct((t, w), BF16),
        grid=(batch, ns),
        in_specs=[pl.BlockSpec((ts, w2), lambda b, s: (b * ns + s, 0))] + [_param(c, layer) for c in params],
        out_specs=pl.BlockSpec((ts, w), lambda b, s: (b * ns + s, 0)),
        scratch_shapes=[pltpu.VMEM((_TAIL + ts, w), F32), pltpu.VMEM((8, w), F32)],
        compiler_params=_cparams("arbitrary", "arbitrary"),
        name="lru",
    )(ug, *params)


def _fox_aug_consts():
    pairs = ATT_HEADS // 2
    place = np.zeros((3, LANES, 2 * pairs * LANES), np.float32)
    bias = np.zeros((1, 2 * pairs * LANES), np.float32)
    for h in range(ATT_HEADS):
        pr, e = divmod(h, 2)
        qcol = pr * LANES + AUG_STRIDE * e
        kcol = (pairs + pr) * LANES + AUG_STRIDE * e
        for term in range(3):
            place[term, h, qcol + term] = 1.0
            place[term, h, kcol + 3 + term] = -1.0
        bias[0, qcol + 3:qcol + 6] = 1.0
        bias[0, kcol:kcol + 3] = 1.0
    return jnp.asarray(place.reshape(3 * LANES, -1), BF16), jnp.asarray(bias, F32)


def _fox_prep_kernel(sm_ref, bf_ref, tri3_ref, place_ref, bias_ref, aq_ref, ak_ref, carry_ref, *, sub):
    first = pl.program_id(1) == 0
    ts = sm_ref.shape[0]
    pairs = aq_ref.shape[0]

    @pl.when(first)
    def _():
        carry_ref[...] = jnp.zeros(carry_ref.shape, F32)

    for r in range(ts // sub):
        rows = slice(r * sub, (r + 1) * sub)
        log_f = jax.nn.log_sigmoid(sm_ref[rows, :] + bf_ref[...])
        fcum = _cumsum_rows(tri3_ref[...], log_f) + carry_ref[0:1, :]
        carry_ref[...] = jnp.broadcast_to(fcum[sub - 1:sub, :], carry_ref.shape)
        aug = (_place_lanes(fcum * LOG2E, place_ref[...]) + bias_ref[...]).astype(BF16)
        for pr in range(pairs):
            aq_ref[pr, rows, :] = aug[:, pr * LANES:(pr + 1) * LANES]
            ak_ref[pr, rows, :] = aug[:, (pairs + pr) * LANES:(pairs + pr + 1) * LANES]


def _fox_prep(small, bf, *, layer, batch, ts=1024, sub=256):
    t = small.shape[0]
    ns = t // batch // ts
    pairs = ATT_HEADS // 2
    place, bias = _fox_aug_consts()
    tri3 = _tri3(sub)
    out = pl.BlockSpec((pairs, ts, LANES), lambda b, i: (0, b * ns + i, 0))
    return pl.pallas_call(
        functools.partial(_fox_prep_kernel, sub=sub),
        out_shape=[jax.ShapeDtypeStruct((pairs, t, LANES), BF16)] * 2,
        grid=(batch, ns),
        in_specs=[pl.BlockSpec((ts, LANES), lambda b, i: (b * ns + i, 0)), _param(bf, layer),
                  _resident(tri3.shape), _resident(place.shape), _resident(bias.shape)],
        out_specs=[out, out],
        scratch_shapes=[pltpu.VMEM((8, LANES), F32)],
        compiler_params=_cparams("arbitrary", "arbitrary"),
        name="fox_prep",
    )(small, bf, tri3, place, bias)


def _fox_kernel(q_ref, aq_ref, k_ref, ak_ref, ve_ref, vo_ref, nm_ref, o_ref,
                acc_ref, m_ref, s_ref, p_ref, al_ref, mx_ref, *, tq):
    tk = tq
    qi = pl.program_id(2)
    q2 = q_ref[...]
    aq = aq_ref[...]
    lane = lax.broadcasted_iota(jnp.int32, (tq, LANES), 1)
    low = lane < HEAD_DIM
    zero = jnp.zeros_like(q2)
    qs = []
    for e in range(2):
        own = jnp.where(low, q2, zero) if e == 0 else jnp.where(low, zero, q2)
        aug = jnp.where((lane >= AUG_STRIDE * e) & (lane < AUG_STRIDE * (e + 1)), aq, zero)
        qs.append(jnp.concatenate([own, aug], axis=1))
    v_refs = (ve_ref, vo_ref)
    reps = tk // LANES

    def rows_of(n):
        return pl.ds(pl.multiple_of(n * tk, tk), tk)

    def keys(n):
        return jnp.concatenate([k_ref[rows_of(n), :], ak_ref[rows_of(n), :]], axis=1)

    def scores(h, kb):
        return lax.dot_general(qs[h], kb, (((1,), (1,)), ((), ())), preferred_element_type=F32)

    def accumulate(h, slot, n):
        acc_ref[h] = al_ref[slot, h] * acc_ref[h] + _dot(p_ref[slot, h], v_refs[h][rows_of(n), :])

    def produce(slot, h, kb):
        s = scores(h, kb)
        s_ref[slot, h] = s
        mx_ref[slot, h] = jnp.broadcast_to(jnp.max(s, axis=1, keepdims=True), (tq, LANES))

    def stage(cur, n, mask_ref=None, lookahead=True):
        prv = 1 - cur
        if lookahead:
            kb = keys(n + 1)
        for h in range(2):
            s = s_ref[cur, h]
            m_prev = m_ref[h]
            if mask_ref is None:
                m_new = jnp.maximum(m_prev, mx_ref[cur, h])
            else:
                s = s + mask_ref[...]
                m_new = jnp.maximum(m_prev, jnp.max(s, axis=1, keepdims=True))
            al_ref[cur, h] = jnp.exp2(m_prev - m_new)
            m_ref[h] = m_new
            p_ref[cur, h] = jnp.exp2(s - jnp.tile(m_new, (1, reps))).astype(BF16)
            if lookahead:
                produce(prv, h, kb)
            accumulate(h, prv, jnp.maximum(n - 1, 0))

    acc_ref[...] = jnp.zeros(acc_ref.shape, F32)
    m_ref[...] = jnp.full(m_ref.shape, -jnp.inf, F32)
    p_ref[1] = jnp.zeros(p_ref.shape[1:], BF16)
    al_ref[1] = jnp.zeros(al_ref.shape[1:], F32)
    kb = keys(0)
    for h in range(2):
        produce(0, h, kb)

    def body(j, carry):
        stage(0, 2 * j)
        stage(1, 2 * j + 1)
        return carry

    lax.fori_loop(0, qi // 2, body, 0)

    @pl.when(qi % 2 == 0)
    def _():
        stage(0, qi, mask_ref=nm_ref, lookahead=False)
        for h in range(2):
            accumulate(h, 0, qi)

    @pl.when(qi % 2 == 1)
    def _():
        stage(0, qi - 1)
        stage(1, qi, mask_ref=nm_ref, lookahead=False)
        for h in range(2):
            accumulate(h, 1, qi)

    num = jnp.where(low, acc_ref[0], acc_ref[1])
    den = pltpu.roll(jnp.where(low, acc_ref[1], acc_ref[0]), HEAD_DIM, axis=1)
    o_ref[...] = (num / den).astype(o_ref.dtype)


def _fox(q, k, ve, vo, aq, ak, *, batch, tq=512):
    t, aw = q.shape
    s = t // batch
    nq = s // tq
    pairs = aw // LANES
    nm = _neg_mask(tq)
    qblk = pl.BlockSpec((tq, LANES), lambda b, hp, i: (b * nq + i, hp))
    kv = pl.BlockSpec((s, LANES), lambda b, hp, i: (b, hp))
    return pl.pallas_call(
        functools.partial(_fox_kernel, tq=tq),
        out_shape=jax.ShapeDtypeStruct((t, aw), BF16),
        grid=(batch, pairs, nq),
        in_specs=[qblk,
                  pl.BlockSpec((None, tq, LANES), lambda b, hp, i: (hp, b * nq + i, 0)),
                  kv,
                  pl.BlockSpec((None, s, LANES), lambda b, hp, i: (hp, b, 0)),
                  kv, kv, _resident(nm.shape)],
        out_specs=qblk,
        scratch_shapes=[pltpu.VMEM((2, tq, LANES), F32), pltpu.VMEM((2, tq, LANES), F32),
                        pltpu.VMEM((2, 2, tq, tq), F32), pltpu.VMEM((2, 2, tq, tq), BF16),
                        pltpu.VMEM((2, 2, tq, LANES), F32), pltpu.VMEM((2, 2, tq, LANES), F32)],
        compiler_params=_cparams("parallel", "parallel", "arbitrary"),
        name="fox",
    )(q, aq, k, ak, ve, vo, nm)


def _ssd_kernel(xbc_ref, z_ref, sm_ref, cw_ref, cb_ref, dtb_ref, alog_ref, dskip_ref, ng_ref,
                tri3_ref, nm_ref, exp3_ref, o_ref, buf_ref, xs_ref, dt_ref, a_ref, state_ref):
    first = pl.program_id(1) == 0
    ts = xbc_ref.shape[0]
    width = z_ref.shape[1]
    gw = width // SSD_GROUPS
    L = SSD_CHUNK
    nst = SSD_GROUPS * SSD_STATE

    @pl.when(first)
    def _():
        state_ref[...] = jnp.zeros(state_ref.shape, F32)

    conv = _causal_conv(buf_ref, xbc_ref[...], cw_ref[...], cb_ref[...], first)
    xs_ref[...] = conv * jax.nn.sigmoid(conv)
    dt = jax.nn.softplus(sm_ref[...] + dtb_ref[...])
    dt_ref[...] = dt
    a_ref[...] = -jnp.exp(alog_ref[...]) * dt

    low = lax.broadcasted_iota(jnp.int32, (L, LANES), 1) < HEAD_DIM

    def chunk(c, carry):
        rows = pl.ds(pl.multiple_of(c * L, L), L)
        xs = xs_ref[rows, 0:width]
        bm = xs_ref[rows, width:width + nst].astype(BF16)
        cm = xs_ref[rows, width + nst:width + 2 * nst].astype(BF16)
        a_cum = _cumsum_rows(tri3_ref[...], a_ref[rows, :])
        a_cum_t = a_cum.T
        a_exp = _place_lanes(a_cum, exp3_ref[...])
        xdt = xs * _place_lanes(dt_ref[rows, :], exp3_ref[...])
        decay_in = jnp.exp(a_exp)
        a_last = a_exp[L - 1:L, :]
        xdec = (jnp.exp(a_last - a_exp) * xdt).astype(BF16)
        chunk_decay = jnp.exp(a_last)
        zero = jnp.zeros((L, LANES), F32)
        y_parts = []
        for g in range(SSD_GROUPS):
            bg = bm[:, g * SSD_STATE:(g + 1) * SSD_STATE]
            cg = cm[:, g * SSD_STATE:(g + 1) * SSD_STATE]
            gmat = lax.dot_general(cg, bg, (((1,), (1,)), ((), ())), preferred_element_type=F32)
            for pr in range(gw // LANES):
                col = g * gw + pr * LANES
                hsl = slice(col, col + LANES)
                hd = col // HEAD_DIM
                blk = a_exp[:, hsl]
                swp = pltpu.roll(blk, HEAD_DIM, axis=1)
                cols = (jnp.where(low, blk, swp), jnp.where(low, swp, blk))
                ms = []
                for e in range(2):
                    r = DT_LANE0 + hd + e
                    seg = cols[e] - a_cum_t[r:r + 1, :]
                    ms.append((gmat * jnp.exp(seg + nm_ref[...])).astype(BF16))
                xp = xdt[:, hsl]
                rhs = jnp.concatenate([jnp.where(low, xp, zero).astype(BF16),
                                       jnp.where(low, zero, xp).astype(BF16)], axis=0)
                y_diag = _dot(jnp.concatenate(ms, axis=1), rhs)
                st = state_ref[:, hsl]
                y_off = _dot(cg, st.astype(BF16)) * decay_in[:, hsl]
                new = lax.dot_general(bg, xdec[:, hsl], (((0,), (0,)), ((), ())),
                                      preferred_element_type=F32)
                state_ref[:, hsl] = chunk_decay[:, hsl] * st + new
                y_parts.append(y_diag + y_off)
        y = jnp.concatenate(y_parts, axis=1) + dskip_ref[...] * xs
        zc = z_ref[rows, :]
        y = y * (zc * jax.nn.sigmoid(zc))
        outs = []
        for g in range(SSD_GROUPS):
            yg = y[:, g * gw:(g + 1) * gw]
            ms_ = jnp.mean(yg * yg, axis=-1, keepdims=True)
            outs.append(yg * lax.rsqrt(ms_ + RMS_EPS))
        o_ref[rows, :] = (jnp.concatenate(outs, axis=1) * ng_ref[...]).astype(o_ref.dtype)
        return carry

    lax.fori_loop(0, ts // L, chunk, 0)


def _ssd(xbc, z, small, params, *, layer, batch, ts=1024):
    t, cdim = xbc.shape
    width = z.shape[1]
    ns = t // batch // ts
    tri3 = _tri3(SSD_CHUNK)
    nm = _neg_mask(SSD_CHUNK)
    src = np.arange(LANES)[:, None]
    expand = (src == DT_LANE0 + np.arange(width)[None, :] // HEAD_DIM).astype(np.float32)
    exp3 = jnp.asarray(np.concatenate([expand] * 3, axis=0), BF16)

    def row(n):
        return pl.BlockSpec((ts, n), lambda b, s: (b * ns + s, 0))

    consts = [tri3, nm, exp3]
    return pl.pallas_call(
        _ssd_kernel,
        out_shape=jax.ShapeDtypeStruct((t, width), BF16),
        grid=(batch, ns),
        in_specs=([row(cdim), row(width), row(LANES)] + [_param(c, layer) for c in params]
                  + [_resident(c.shape) for c in consts]),
        out_specs=row(width),
        scratch_shapes=[pltpu.VMEM((_TAIL + ts, cdim), F32), pltpu.VMEM((ts, cdim), F32),
                        pltpu.VMEM((ts, LANES), F32), pltpu.VMEM((ts, LANES), F32),
                        pltpu.VMEM((SSD_STATE, width), F32)],
        compiler_params=_cparams("arbitrary", "arbitrary"),
        name="ssd",
    )(xbc, z, small, *params, *consts)


def _post_kernel(x_ref, ya_ref, yb_ref, yc_ref, p_ref, wo_ref, g2_ref, b2_ref, wg_ref, wu_ref, wd_ref,
                 g3_ref, b3_ref, pgw_ref, pgb_ref, ppw_ref, o_ref, *, alpha):
    na, nb = ya_ref.shape[1], yb_ref.shape[1]
    mix = (_dot(ya_ref[...], wo_ref[0:na, :]) + _dot(yb_ref[...], wo_ref[na:na + nb, :])
           + _dot(yc_ref[...], wo_ref[na + nb:, :]))
    x2 = _layer_norm(alpha * x_ref[...] + mix, g2_ref[...], b2_ref[...])
    x3 = _ffn_ln_tile(x2, wg_ref, wu_ref, wd_ref, g3_ref, b3_ref, alpha)
    gate = jax.nn.sigmoid(_dot(x3.astype(BF16), pgw_ref[...]) + pgb_ref[...])
    o_ref[...] = x3 + gate * _dot(p_ref[...].astype(BF16), ppw_ref[...])


def _post(x, ya, yb, yc, p, params, *, layer, alpha, tm=512):
    t, d = x.shape

    def row(n):
        return pl.BlockSpec((tm, n), lambda i: (i, 0))

    return pl.pallas_call(
        functools.partial(_post_kernel, alpha=alpha),
        out_shape=jax.ShapeDtypeStruct((t, d), F32),
        grid=(t // tm,),
        in_specs=([row(d), row(ya.shape[1]), row(yb.shape[1]), row(yc.shape[1]),
                   pl.BlockSpec((None, tm, p.shape[2]), lambda i: (layer, i, 0))]
                  + [_param(c, layer) for c in params]),
        out_specs=row(d),
        compiler_params=_cparams("parallel"),
        name="post",
    )(x, ya, yb, yc, p, *params)


def _block_diag(w):
    d, h, i, j = w.shape
    eye = jnp.eye(h, dtype=w.dtype)
    return (eye[None, :, None, :, None] * w[:, :, :, None, :]).reshape(d, h * i, h * j)


def _rows(v):
    return v[:, None, :].astype(F32)


def _head_lanes(v, offset):
    d, h = v.shape
    return jnp.zeros((d, 1, LANES), F32).at[:, 0, offset:offset + h].set(v.astype(F32))


def kernel(x, p, ln1_g, ln1_b, ffn1_wg, ffn1_wu, ffn1_wd, w_in, lru_conv_w, lru_conv_b, lru_wa, lru_ba, lru_wx, lru_bx, lru_lambda, fox_bf, ssd_conv_w, ssd_conv_b, ssd_dt_bias, ssd_a_log, ssd_d, ssd_norm_g, w_out, ln2_g, ln2_b, ffn2_wg, ffn2_wu, ffn2_wd, ln3_g, ln3_b, pe_proj, pe_gate_w, pe_gate_b):
    batch, seq, d_model = x.shape
    depth = p.shape[0]
    t = batch * seq
    alpha = (2.0 * depth) ** 0.25
    lru_w = lru_conv_w.shape[-1]
    att_w = ATT_HEADS * HEAD_DIM
    ssd_w = SSD_HEADS * HEAD_DIM
    conv_dim = ssd_conv_w.shape[-1]

    sizes = (lru_w, lru_w, att_w, att_w, att_w, ATT_HEADS, ssd_w, conv_dim, SSD_HEADS)
    offs = [0]
    for s_ in sizes:
        offs.append(offs[-1] + s_)
    col = lambda j: w_in[:, :, offs[j]:offs[j + 1]]
    pad = jnp.zeros((depth, d_model, LANES - ATT_HEADS - SSD_HEADS), F32)
    w_cat = jnp.concatenate([col(0), col(1), col(2), col(3), col(4), col(6), col(7),
                             col(5), col(8), pad], axis=2).astype(BF16)

    pre_params = [ffn1_wg.astype(BF16), ffn1_wu.astype(BF16), ffn1_wd.astype(BF16),
                  _rows(ln1_g), _rows(ln1_b), w_cat]
    lru_params = [lru_conv_w, _rows(lru_conv_b), _block_diag(lru_wa).astype(BF16), _rows(lru_ba),
                  _block_diag(lru_wx).astype(BF16), _rows(lru_bx), _rows(lru_lambda)]
    fox_bias = _head_lanes(fox_bf, 0)
    ssd_params = [ssd_conv_w, _rows(ssd_conv_b), _head_lanes(ssd_dt_bias, DT_LANE0),
                  _head_lanes(ssd_a_log, DT_LANE0), _rows(jnp.repeat(ssd_d, HEAD_DIM, axis=1)),
                  _rows(ssd_norm_g)]
    post_params = [w_out.astype(BF16), _rows(ln2_g), _rows(ln2_b),
                   ffn2_wg.astype(BF16), ffn2_wu.astype(BF16), ffn2_wd.astype(BF16),
                   _rows(ln3_g), _rows(ln3_b),
                   pe_gate_w.astype(BF16), _rows(pe_gate_b), pe_proj.astype(BF16)]
    pf = p.reshape(depth, t, p.shape[-1])

    xf = x.reshape(t, d_model)
    for i in range(depth):
        xf, ug, q, k, ve, vo, z, xbc, small = _pre(xf, pre_params, layer=i, alpha=alpha)
        y_a = _lru(ug, lru_params, layer=i, batch=batch)
        aq, ak = _fox_prep(small, fox_bias, layer=i, batch=batch)
        y_b = _fox(q, k, ve, vo, aq, ak, batch=batch)
        y_c = _ssd(xbc, z, small, ssd_params, layer=i, batch=batch)
        xf = _post(xf, y_a, y_b, y_c, pf, post_params, layer=i, alpha=alpha)
    return xf.reshape(batch, seq, d_model)
```

```python
import functools
import math

import numpy as np
import jax
import jax.numpy as jnp
from jax import lax
from jax.experimental import pallas as pl
from jax.experimental.pallas import tpu as pltpu

F32 = jnp.float32
BF16 = jnp.bfloat16

LANES = 128
HEAD_DIM = 64
LRU_C = 8.0
CONV_K = 4
SSD_CHUNK = 128
SSD_STATE = 128
SSD_HEADS = 8
SSD_GROUPS = 2
ATT_HEADS = 4
DT_LANE0 = ATT_HEADS
AUG_STRIDE = 8
LN_EPS = 1e-5
RMS_EPS = 1e-5
LOG2E = math.log2(math.e)
VMEM_LIMIT = 56 * 1024 * 1024


def _cparams(*sem):
    return pltpu.CompilerParams(dimension_semantics=sem, vmem_limit_bytes=VMEM_LIMIT)


def _resident(shape):
    nd = len(shape)
    return pl.BlockSpec(shape, lambda *_: (0,) * nd, pipeline_mode=pl.Buffered(1))


def _param(arr, layer):
    nd = arr.ndim
    return pl.BlockSpec((None,) + arr.shape[1:], lambda *_: (layer,) + (0,) * (nd - 1),
                        pipeline_mode=pl.Buffered(1))


def _layer_norm(y, g, b):
    mu = jnp.mean(y, axis=-1, keepdims=True)
    d = y - mu
    var = jnp.mean(d * d, axis=-1, keepdims=True)
    return d * lax.rsqrt(var + LN_EPS) * g + b


def _dot(a, b):
    return jnp.dot(a, b, preferred_element_type=F32)


def _split3(x):
    hi = x.astype(BF16)
    r = x - hi.astype(F32)
    mid = r.astype(BF16)
    lo = (r - mid.astype(F32)).astype(BF16)
    return hi, mid, lo


def _cumsum_rows(tri3, x):
    return _dot(tri3, jnp.concatenate(_split3(x), axis=0))


def _place_lanes(x, w3):
    return _dot(jnp.concatenate(_split3(x), axis=1), w3)


FFN_CHUNK = 512


def _ffn_ln_tile(x, wg_ref, wu_ref, wd_ref, g_ref, b_ref, alpha):
    xb = x.astype(BF16)
    ffn = wg_ref.shape[1]
    acc = jnp.zeros(x.shape, F32)
    for c0 in range(0, ffn, FFN_CHUNK):
        c1 = min(c0 + FFN_CHUNK, ffn)
        g = _dot(xb, wg_ref[:, c0:c1])
        u = _dot(xb, wu_ref[:, c0:c1])
        h = (g * jax.nn.sigmoid(g) * u).astype(BF16)
        acc = acc + _dot(h, wd_ref[c0:c1, :])
    return _layer_norm(alpha * x + 0.5 * acc, g_ref[...], b_ref[...])


_IN_XBC, _IN_UG, _IN_QKV, _IN_Z, _IN_SMALL = 1024, 512, 768, 512, LANES
_TAIL = 8


def _causal_conv(buf_ref, x, w, b, first):
    ts = x.shape[0]
    buf_ref[0:_TAIL, :] = jnp.where(first, 0.0, buf_ref[0:_TAIL, :])
    buf_ref[_TAIL:_TAIL + ts, :] = x
    ext = buf_ref[...]
    y = b + w[CONV_K - 1:CONV_K, :] * x
    for j in range(CONV_K - 1):
        back = CONV_K - 1 - j
        y = y + w[j:j + 1, :] * pltpu.roll(ext, back, axis=0)[_TAIL:_TAIL + ts, :]
    buf_ref[0:_TAIL, :] = x[ts - _TAIL:ts, :]
    return y


def _lru_terms(u, wa_ref, ba_ref, wx_ref, bx_ref, lam_ref):
    ub = u.astype(BF16)
    r = jax.nn.sigmoid(_dot(ub, wa_ref[...]) + ba_ref[...])
    ig = jax.nn.sigmoid(_dot(ub, wx_ref[...]) + bx_ref[...])
    log_a = -LRU_C * r * jax.nn.softplus(-lam_ref[...])
    a = jnp.exp(log_a)
    b = jnp.sqrt(-jnp.tanh(log_a) * (a * a + 1.0)) * (ig * u)
    return a, b


def _pre_kernel(x_ref, wg_ref, wu_ref, wd_ref, g_ref, b_ref, w_ref,
                scw_ref, scb_ref, lcw_ref, lcb_ref, wa_ref, ba_ref, wx_ref, bx_ref, lam_ref,
                xo_ref, xs_ref, la_ref, lb_ref, lg_ref, q_ref, k_ref, ve_ref, vo_ref, z_ref, sm_ref,
                sbuf_ref, lbuf_ref, raw_ref, *, alpha, tiles_per_seq):
    i = pl.program_id(0)
    lw = _IN_UG // 2

    @pl.when(i == 0)
    def _():
        raw_ref[...] = jnp.zeros(raw_ref.shape, F32)
        sbuf_ref[0:_TAIL, :] = jnp.zeros((_TAIL, sbuf_ref.shape[1]), F32)
        lbuf_ref[0:_TAIL, :] = jnp.zeros((_TAIL, lbuf_ref.shape[1]), F32)

    first = (i + tiles_per_seq - 1) % tiles_per_seq == 0
    conv = _causal_conv(sbuf_ref, raw_ref[:, 0:_IN_XBC], scw_ref[...], scb_ref[...], first)
    xs_ref[...] = conv * jax.nn.sigmoid(conv)
    u = _causal_conv(lbuf_ref, raw_ref[:, _IN_XBC:_IN_XBC + lw], lcw_ref[...], lcb_ref[...], first)
    a, b = _lru_terms(u, wa_ref, ba_ref, wx_ref, bx_ref, lam_ref)
    la_ref[...] = a
    lb_ref[...] = b
    lg_ref[...] = jax.nn.gelu(raw_ref[:, _IN_XBC + lw:_IN_XBC + 2 * lw])

    x1 = _ffn_ln_tile(x_ref[...], wg_ref, wu_ref, wd_ref, g_ref, b_ref, alpha)
    xo_ref[...] = x1
    hh = _dot(x1.astype(BF16), w_ref[...])
    raw_ref[...] = hh[:, 0:_IN_XBC + _IN_UG]
    h = hh[:, _IN_XBC + _IN_UG:]
    o = 0
    aw = _IN_QKV // 3
    q = h[:, o:o + aw]
    k = h[:, o + aw:o + 2 * aw]
    v = h[:, o + 2 * aw:o + 3 * aw]
    o += _IN_QKV
    q_ref[...] = (q * (HEAD_DIM ** -0.5 * LOG2E)).astype(BF16)
    k_ref[...] = k.astype(BF16)
    even = (lax.broadcasted_iota(jnp.int32, v.shape, 1) % LANES) < HEAD_DIM
    ve_ref[...] = jnp.where(even, v, 1.0).astype(BF16)
    vo_ref[...] = jnp.where(even, 1.0, v).astype(BF16)
    z_ref[...] = h[:, o:o + _IN_Z]
    o += _IN_Z
    sm_ref[...] = h[:, o:o + _IN_SMALL]


def _pre(x, params, *, layer, alpha, seq, tm=512):
    t, d = x.shape
    aw = _IN_QKV // 3
    lw = _IN_UG // 2

    nt = t // tm

    def cur(n):
        return pl.BlockSpec((tm, n), lambda i: (jnp.minimum(i, nt - 1), 0))

    def prev(n):
        return pl.BlockSpec((tm, n), lambda i: (jnp.maximum(i - 1, 0), 0))

    outs = [(d, F32, cur), (_IN_XBC, F32, prev), (lw, F32, prev), (lw, F32, prev), (lw, F32, prev),
            (aw, BF16, cur), (aw, BF16, cur), (aw, BF16, cur), (aw, BF16, cur),
            (_IN_Z, F32, cur), (_IN_SMALL, F32, cur)]
    return pl.pallas_call(
        functools.partial(_pre_kernel, alpha=alpha, tiles_per_seq=seq // tm),
        out_shape=[jax.ShapeDtypeStruct((t, n), dt) for n, dt, _ in outs],
        grid=(nt + 1,),
        in_specs=[cur(d)] + [_param(c, layer) for c in params],
        out_specs=[spec(n) for n, _, spec in outs],
        scratch_shapes=[pltpu.VMEM((_TAIL + tm, _IN_XBC), F32), pltpu.VMEM((_TAIL + tm, lw), F32),
                        pltpu.VMEM((tm, _IN_XBC + _IN_UG), F32)],
        compiler_params=_cparams("arbitrary"),
        name="pre",
    )(x, *params)


def _tri3(n):
    tri = np.tril(np.ones((n, n), np.float32))
    return jnp.asarray(np.concatenate([tri, tri, tri], axis=1), BF16)


def _neg_mask(n):
    return jnp.asarray(np.where(np.tril(np.ones((n, n), bool)), 0.0, -np.inf), F32)


def _lru_kernel(a_ref, b_ref, g_ref, o_ref, h_ref):
    first = pl.program_id(1) == 0
    ts = a_ref.shape[0]
    a = a_ref[...]
    b = b_ref[...]

    row = lax.broadcasted_iota(jnp.int32, a.shape, 0)
    d = 1
    while d < ts:
        keep = row >= d
        a_s = jnp.where(keep, pltpu.roll(a, d, axis=0), 1.0)
        b_s = jnp.where(keep, pltpu.roll(b, d, axis=0), 0.0)
        b = a * b_s + b
        a = a * a_s
        d *= 2

    @pl.when(first)
    def _():
        h_ref[...] = jnp.zeros(h_ref.shape, F32)

    h = b + a * h_ref[0:1, :]
    h_ref[...] = jnp.broadcast_to(h[ts - 1:ts, :], h_ref.shape)
    o_ref[...] = (h * g_ref[...]).astype(o_ref.dtype)


def _lru(a, b, g, *, batch, ts=256):
    t, w = a.shape
    ns = t // batch // ts
    blk = pl.BlockSpec((ts, w), lambda bi, s: (bi * ns + s, 0))
    return pl.pallas_call(
        _lru_kernel,
        out_shape=jax.ShapeDtypeStruct((t, w), BF16),
        grid=(batch, ns),
        in_specs=[blk, blk, blk],
        out_specs=blk,
        scratch_shapes=[pltpu.VMEM((8, w), F32)],
        compiler_params=_cparams("arbitrary", "arbitrary"),
        name="lru",
    )(a, b, g)


def _fox_aug_consts():
    pairs = ATT_HEADS // 2
    place = np.zeros((3, LANES, 2 * pairs * LANES), np.float32)
    bias = np.zeros((1, 2 * pairs * LANES), np.float32)
    for h in range(ATT_HEADS):
        pr, e = divmod(h, 2)
        qcol = pr * LANES + AUG_STRIDE * e
        kcol = (pairs + pr) * LANES + AUG_STRIDE * e
        for term in range(3):
            place[term, h, qcol + term] = 1.0
            place[term, h, kcol + 3 + term] = -1.0
        bias[0, qcol + 3:qcol + 6] = 1.0
        bias[0, kcol:kcol + 3] = 1.0
    return jnp.asarray(place.reshape(3 * LANES, -1), BF16), jnp.asarray(bias, F32)


def _fox_prep_kernel(sm_ref, bf_ref, tri3_ref, place_ref, bias_ref, aq_ref, ak_ref, carry_ref, *, sub):
    first = pl.program_id(1) == 0
    ts = sm_ref.shape[0]
    pairs = aq_ref.shape[0]

    @pl.when(first)
    def _():
        carry_ref[...] = jnp.zeros(carry_ref.shape, F32)

    for r in range(ts // sub):
        rows = slice(r * sub, (r + 1) * sub)
        log_f = jax.nn.log_sigmoid(sm_ref[rows, :] + bf_ref[...])
        fcum = _cumsum_rows(tri3_ref[...], log_f) + carry_ref[0:1, :]
        carry_ref[...] = jnp.broadcast_to(fcum[sub - 1:sub, :], carry_ref.shape)
        aug = (_place_lanes(fcum * LOG2E, place_ref[...]) + bias_ref[...]).astype(BF16)
        for pr in range(pairs):
            aq_ref[pr, rows, :] = aug[:, pr * LANES:(pr + 1) * LANES]
            ak_ref[pr, rows, :] = aug[:, (pairs + pr) * LANES:(pairs + pr + 1) * LANES]


def _fox_prep(small, bf, *, layer, batch, ts=1024, sub=256):
    t = small.shape[0]
    ns = t // batch // ts
    pairs = ATT_HEADS // 2
    place, bias = _fox_aug_consts()
    tri3 = _tri3(sub)
    out = pl.BlockSpec((pairs, ts, LANES), lambda b, i: (0, b * ns + i, 0))
    return pl.pallas_call(
        functools.partial(_fox_prep_kernel, sub=sub),
        out_shape=[jax.ShapeDtypeStruct((pairs, t, LANES), BF16)] * 2,
        grid=(batch, ns),
        in_specs=[pl.BlockSpec((ts, LANES), lambda b, i: (b * ns + i, 0)), _param(bf, layer),
                  _resident(tri3.shape), _resident(place.shape), _resident(bias.shape)],
        out_specs=[out, out],
        scratch_shapes=[pltpu.VMEM((8, LANES), F32)],
        compiler_params=_cparams("arbitrary", "arbitrary"),
        name="fox_prep",
    )(small, bf, tri3, place, bias)


def _fox_kernel(q_ref, aq_ref, k_ref, ak_ref, ve_ref, vo_ref, nm_ref, o_ref,
                acc_ref, m_ref, s_ref, p_ref, al_ref, mx_ref, *, tq):
    tk = tq
    qi = pl.program_id(2)
    q2 = q_ref[...]
    aq = aq_ref[...]
    lane = lax.broadcasted_iota(jnp.int32, (tq, LANES), 1)
    low = lane < HEAD_DIM
    zero = jnp.zeros_like(q2)
    qs = []
    for e in range(2):
        own = jnp.where(low, q2, zero) if e == 0 else jnp.where(low, zero, q2)
        aug = jnp.where((lane >= AUG_STRIDE * e) & (lane < AUG_STRIDE * (e + 1)), aq, zero)
        qs.append(jnp.concatenate([own, aug], axis=1))
    v_refs = (ve_ref, vo_ref)
    reps = tk // LANES

    def rows_of(n):
        return pl.ds(pl.multiple_of(n * tk, tk), tk)

    def keys(n):
        return jnp.concatenate([k_ref[rows_of(n), :], ak_ref[rows_of(n), :]], axis=1)

    def scores(h, kb):
        return lax.dot_general(qs[h], kb, (((1,), (1,)), ((), ())), preferred_element_type=F32)

    def accumulate(h, slot, n):
        acc_ref[h] = al_ref[slot, h] * acc_ref[h] + _dot(p_ref[slot, h], v_refs[h][rows_of(n), :])

    def produce(slot, h, kb):
        s = scores(h, kb)
        s_ref[slot, h] = s
        mx_ref[slot, h] = jnp.broadcast_to(jnp.max(s, axis=1, keepdims=True), (tq, LANES))

    def stage(cur, n, mask_ref=None, lookahead=True):
        prv = 1 - cur
        if lookahead:
            kb = keys(n + 1)
        for h in range(2):
            s = s_ref[cur, h]
            m_prev = m_ref[h]
            if mask_ref is None:
                m_new = jnp.maximum(m_prev, mx_ref[cur, h])
            else:
                s = s + mask_ref[...]
                m_new = jnp.maximum(m_prev, jnp.max(s, axis=1, keepdims=True))
            al_ref[cur, h] = jnp.exp2(m_prev - m_new)
            m_ref[h] = m_new
            p_ref[cur, h] = jnp.exp2(s - jnp.tile(m_new, (1, reps))).astype(BF16)
            if lookahead:
                produce(prv, h, kb)
            accumulate(h, prv, jnp.maximum(n - 1, 0))

    acc_ref[...] = jnp.zeros(acc_ref.shape, F32)
    m_ref[...] = jnp.full(m_ref.shape, -jnp.inf, F32)
    p_ref[1] = jnp.zeros(p_ref.shape[1:], BF16)
    al_ref[1] = jnp.zeros(al_ref.shape[1:], F32)
    kb = keys(0)
    for h in range(2):
        produce(0, h, kb)

    def body(j, carry):
        stage(0, 2 * j)
        stage(1, 2 * j + 1)
        return carry

    lax.fori_loop(0, qi // 2, body, 0)

    @pl.when(qi % 2 == 0)
    def _():
        stage(0, qi, mask_ref=nm_ref, lookahead=False)
        for h in range(2):
            accumulate(h, 0, qi)

    @pl.when(qi % 2 == 1)
    def _():
        stage(0, qi - 1)
        stage(1, qi, mask_ref=nm_ref, lookahead=False)
        for h in range(2):
            accumulate(h, 1, qi)

    num = jnp.where(low, acc_ref[0], acc_ref[1])
    den = pltpu.roll(jnp.where(low, acc_ref[1], acc_ref[0]), HEAD_DIM, axis=1)
    o_ref[...] = (num / den).astype(o_ref.dtype)


def _fox(q, k, ve, vo, aq, ak, *, batch, tq=512):
    t, aw = q.shape
    s = t // batch
    nq = s // tq
    pairs = aw // LANES
    nm = _neg_mask(tq)
    qblk = pl.BlockSpec((tq, LANES), lambda b, hp, i: (b * nq + i, hp))
    kv = pl.BlockSpec((s, LANES), lambda b, hp, i: (b, hp))
    return pl.pallas_call(
        functools.partial(_fox_kernel, tq=tq),
        out_shape=jax.ShapeDtypeStruct((t, aw), BF16),
        grid=(batch, pairs, nq),
        in_specs=[qblk,
                  pl.BlockSpec((None, tq, LANES), lambda b, hp, i: (hp, b * nq + i, 0)),
                  kv,
                  pl.BlockSpec((None, s, LANES), lambda b, hp, i: (hp, b, 0)),
                  kv, kv, _resident(nm.shape)],
        out_specs=qblk,
        scratch_shapes=[pltpu.VMEM((2, tq, LANES), F32), pltpu.VMEM((2, tq, LANES), F32),
                        pltpu.VMEM((2, 2, tq, tq), F32), pltpu.VMEM((2, 2, tq, tq), BF16),
                        pltpu.VMEM((2, 2, tq, LANES), F32), pltpu.VMEM((2, 2, tq, LANES), F32)],
        compiler_params=_cparams("parallel", "parallel", "arbitrary"),
        name="fox",
    )(q, aq, k, ak, ve, vo, nm)


def _ssd_kernel(xs_ref, z_ref, sm_ref, dtb_ref, alog_ref, dskip_ref, ng_ref,
                tri3_ref, nm_ref, exp3_ref, o_ref, dt_ref, a_ref, state_ref):
    first = pl.program_id(1) == 0
    ts = xs_ref.shape[0]
    width = z_ref.shape[1]
    gw = width // SSD_GROUPS
    L = SSD_CHUNK
    nst = SSD_GROUPS * SSD_STATE

    @pl.when(first)
    def _():
        state_ref[...] = jnp.zeros(state_ref.shape, F32)

    dt = jax.nn.softplus(sm_ref[...] + dtb_ref[...])
    dt_ref[...] = dt
    a_ref[...] = -jnp.exp(alog_ref[...]) * dt

    low = lax.broadcasted_iota(jnp.int32, (L, LANES), 1) < HEAD_DIM

    def chunk(c):
        rows = pl.ds(c * L, L)
        xs = xs_ref[rows, 0:width]
        bm = xs_ref[rows, width:width + nst].astype(BF16)
        cm = xs_ref[rows, width + nst:width + 2 * nst].astype(BF16)
        a_cum = _cumsum_rows(tri3_ref[...], a_ref[rows, :])
        a_cum_t = a_cum.T
        a_exp = _place_lanes(a_cum, exp3_ref[...])
        xdt = xs * _place_lanes(dt_ref[rows, :], exp3_ref[...])
        decay_in = jnp.exp(a_exp)
        a_last = a_exp[L - 1:L, :]
        xdec = (jnp.exp(a_last - a_exp) * xdt).astype(BF16)
        chunk_decay = jnp.exp(a_last)
        zero = jnp.zeros((L, LANES), F32)
        y_parts = []
        for g in range(SSD_GROUPS):
            bg = bm[:, g * SSD_STATE:(g + 1) * SSD_STATE]
            cg = cm[:, g * SSD_STATE:(g + 1) * SSD_STATE]
            gmat = lax.dot_general(cg, bg, (((1,), (1,)), ((), ())), preferred_element_type=F32)
            for pr in range(gw // LANES):
                col = g * gw + pr * LANES
                hsl = slice(col, col + LANES)
                hd = col // HEAD_DIM
                blk = a_exp[:, hsl]
                swp = pltpu.roll(blk, HEAD_DIM, axis=1)
                cols = (jnp.where(low, blk, swp), jnp.where(low, swp, blk))
                ms = []
                for e in range(2):
                    r = DT_LANE0 + hd + e
                    seg = cols[e] - a_cum_t[r:r + 1, :]
                    ms.append((gmat * jnp.exp(seg + nm_ref[...])).astype(BF16))
                xp = xdt[:, hsl]
                rhs = jnp.concatenate([jnp.where(low, xp, zero).astype(BF16),
                                       jnp.where(low, zero, xp).astype(BF16)], axis=0)
                y_diag = _dot(jnp.concatenate(ms, axis=1), rhs)
                st = state_ref[:, hsl]
                y_off = _dot(cg, st.astype(BF16)) * decay_in[:, hsl]
                new = lax.dot_general(bg, xdec[:, hsl], (((0,), (0,)), ((), ())),
                                      preferred_element_type=F32)
                state_ref[:, hsl] = chunk_decay[:, hsl] * st + new
                y_parts.append(y_diag + y_off)
        y = jnp.concatenate(y_parts, axis=1) + dskip_ref[...] * xs
        zc = z_ref[rows, :]
        y = y * (zc * jax.nn.sigmoid(zc))
        outs = []
        for g in range(SSD_GROUPS):
            yg = y[:, g * gw:(g + 1) * gw]
            ms_ = jnp.mean(yg * yg, axis=-1, keepdims=True)
            outs.append(yg * lax.rsqrt(ms_ + RMS_EPS))
        o_ref[rows, :] = (jnp.concatenate(outs, axis=1) * ng_ref[...]).astype(o_ref.dtype)

    for c in range(ts // L):
        chunk(c)


def _ssd(xs, z, small, params, *, layer, batch, ts=1024):
    t, cdim = xs.shape
    width = z.shape[1]
    ns = t // batch // ts
    tri3 = _tri3(SSD_CHUNK)
    nm = _neg_mask(SSD_CHUNK)
    src = np.arange(LANES)[:, None]
    expand = (src == DT_LANE0 + np.arange(width)[None, :] // HEAD_DIM).astype(np.float32)
    exp3 = jnp.asarray(np.concatenate([expand] * 3, axis=0), BF16)

    def row(n):
        return pl.BlockSpec((ts, n), lambda b, s: (b * ns + s, 0))

    consts = [tri3, nm, exp3]
    return pl.pallas_call(
        _ssd_kernel,
        out_shape=jax.ShapeDtypeStruct((t, width), BF16),
        grid=(batch, ns),
        in_specs=([row(cdim), row(width), row(LANES)] + [_param(c, layer) for c in params]
                  + [_resident(c.shape) for c in consts]),
        out_specs=row(width),
        scratch_shapes=[pltpu.VMEM((ts, LANES), F32), pltpu.VMEM((ts, LANES), F32),
                        pltpu.VMEM((SSD_STATE, width), F32)],
        compiler_params=_cparams("arbitrary", "arbitrary"),
        name="ssd",
    )(xs, z, small, *params, *consts)


def _post_kernel(x_ref, ya_ref, yb_ref, yc_ref, p_ref, wo_ref, g2_ref, b2_ref, wg_ref, wu_ref, wd_ref,
                 g3_ref, b3_ref, pgw_ref, pgb_ref, ppw_ref, o_ref, *, alpha):
    na, nb = ya_ref.shape[1], yb_ref.shape[1]
    mix = (_dot(ya_ref[...], wo_ref[0:na, :]) + _dot(yb_ref[...], wo_ref[na:na + nb, :])
           + _dot(yc_ref[...], wo_ref[na + nb:, :]))
    x2 = _layer_norm(alpha * x_ref[...] + mix, g2_ref[...], b2_ref[...])
    x3 = _ffn_ln_tile(x2, wg_ref, wu_ref, wd_ref, g3_ref, b3_ref, alpha)
    gate = jax.nn.sigmoid(_dot(x3.astype(BF16), pgw_ref[...]) + pgb_ref[...])
    o_ref[...] = x3 + gate * _dot(p_ref[...].astype(BF16), ppw_ref[...])


def _post(x, ya, yb, yc, p, params, *, layer, alpha, tm=512):
    t, d = x.shape

    def row(n):
        return pl.BlockSpec((tm, n), lambda i: (i, 0))

    return pl.pallas_call(
        functools.partial(_post_kernel, alpha=alpha),
        out_shape=jax.ShapeDtypeStruct((t, d), F32),
        grid=(t // tm,),
        in_specs=([row(d), row(ya.shape[1]), row(yb.shape[1]), row(yc.shape[1]),
                   pl.BlockSpec((None, tm, p.shape[2]), lambda i: (layer, i, 0))]
                  + [_param(c, layer) for c in params]),
        out_specs=row(d),
        compiler_params=_cparams("parallel"),
        name="post",
    )(x, ya, yb, yc, p, *params)


def _block_diag(w):
    d, h, i, j = w.shape
    eye = jnp.eye(h, dtype=w.dtype)
    return (eye[None, :, None, :, None] * w[:, :, :, None, :]).reshape(d, h * i, h * j)


def _rows(v):
    return v[:, None, :].astype(F32)


def _head_lanes(v, offset):
    d, h = v.shape
    return jnp.zeros((d, 1, LANES), F32).at[:, 0, offset:offset + h].set(v.astype(F32))


def kernel(x, p, ln1_g, ln1_b, ffn1_wg, ffn1_wu, ffn1_wd, w_in, lru_conv_w, lru_conv_b, lru_wa, lru_ba, lru_wx, lru_bx, lru_lambda, fox_bf, ssd_conv_w, ssd_conv_b, ssd_dt_bias, ssd_a_log, ssd_d, ssd_norm_g, w_out, ln2_g, ln2_b, ffn2_wg, ffn2_wu, ffn2_wd, ln3_g, ln3_b, pe_proj, pe_gate_w, pe_gate_b):
    batch, seq, d_model = x.shape
    depth = p.shape[0]
    t = batch * seq
    alpha = (2.0 * depth) ** 0.25
    lru_w = lru_conv_w.shape[-1]
    att_w = ATT_HEADS * HEAD_DIM
    ssd_w = SSD_HEADS * HEAD_DIM
    conv_dim = ssd_conv_w.shape[-1]

    sizes = (lru_w, lru_w, att_w, att_w, att_w, ATT_HEADS, ssd_w, conv_dim, SSD_HEADS)
    offs = [0]
    for s_ in sizes:
        offs.append(offs[-1] + s_)
    col = lambda j: w_in[:, :, offs[j]:offs[j + 1]]
    pad = jnp.zeros((depth, d_model, LANES - ATT_HEADS - SSD_HEADS), F32)
    w_cat = jnp.concatenate([col(7), col(0), col(1), col(2), col(3), col(4), col(6),
                             col(5), col(8), pad], axis=2).astype(BF16)

    pre_params = [ffn1_wg.astype(BF16), ffn1_wu.astype(BF16), ffn1_wd.astype(BF16),
                  _rows(ln1_g), _rows(ln1_b), w_cat,
                  ssd_conv_w, _rows(ssd_conv_b), lru_conv_w, _rows(lru_conv_b),
                  _block_diag(lru_wa).astype(BF16), _rows(lru_ba),
                  _block_diag(lru_wx).astype(BF16), _rows(lru_bx), _rows(lru_lambda)]
    fox_bias = _head_lanes(fox_bf, 0)
    ssd_params = [_head_lanes(ssd_dt_bias, DT_LANE0), _head_lanes(ssd_a_log, DT_LANE0),
                  _rows(jnp.repeat(ssd_d, HEAD_DIM, axis=1)), _rows(ssd_norm_g)]
    post_params = [w_out.astype(BF16), _rows(ln2_g), _rows(ln2_b),
                   ffn2_wg.astype(BF16), ffn2_wu.astype(BF16), ffn2_wd.astype(BF16),
                   _rows(ln3_g), _rows(ln3_b),
                   pe_gate_w.astype(BF16), _rows(pe_gate_b), pe_proj.astype(BF16)]
    pf = p.reshape(depth, t, p.shape[-1])

    xf = x.reshape(t, d_model)
    for i in range(depth):
        xf, xs, la, lb, lg, q, k, ve, vo, z, small = _pre(xf, pre_params, layer=i, alpha=alpha, seq=seq)
        y_a = _lru(la, lb, lg, batch=batch)
        aq, ak = _fox_prep(small, fox_bias, layer=i, batch=batch)
        y_b = _fox(q, k, ve, vo, aq, ak, batch=batch)
        y_c = _ssd(xs, z, small, ssd_params, layer=i, batch=batch)
        xf = _post(xf, y_a, y_b, y_c, pf, post_params, layer=i, alpha=alpha)
    return xf.reshape(batch, seq, d_model)
```

```python
import functools
import math

import numpy as np
import jax
import jax.numpy as jnp
from jax import lax
from jax.experimental import pallas as pl
from jax.experimental.pallas import tpu as pltpu

F32 = jnp.float32
BF16 = jnp.bfloat16

LANES = 128
HEAD_DIM = 64
LRU_C = 8.0
CONV_K = 4
SSD_CHUNK = 128
SSD_STATE = 128
SSD_HEADS = 8
SSD_GROUPS = 2
ATT_HEADS = 4
DT_LANE0 = ATT_HEADS
AUG_STRIDE = 8
LN_EPS = 1e-5
RMS_EPS = 1e-5
LOG2E = math.log2(math.e)
VMEM_LIMIT = 56 * 1024 * 1024


def _cparams(*sem):
    return pltpu.CompilerParams(dimension_semantics=sem, vmem_limit_bytes=VMEM_LIMIT)


def _resident(shape):
    nd = len(shape)
    return pl.BlockSpec(shape, lambda *_: (0,) * nd, pipeline_mode=pl.Buffered(1))


def _param(arr, layer):
    nd = arr.ndim
    return pl.BlockSpec((None,) + arr.shape[1:], lambda *_: (layer,) + (0,) * (nd - 1),
                        pipeline_mode=pl.Buffered(1))


def _layer_norm(y, g, b):
    mu = jnp.mean(y, axis=-1, keepdims=True)
    d = y - mu
    var = jnp.mean(d * d, axis=-1, keepdims=True)
    return d * lax.rsqrt(var + LN_EPS) * g + b


def _dot(a, b):
    return jnp.dot(a, b, preferred_element_type=F32)


def _split3(x):
    hi = x.astype(BF16)
    r = x - hi.astype(F32)
    mid = r.astype(BF16)
    lo = (r - mid.astype(F32)).astype(BF16)
    return hi, mid, lo


def _cumsum_rows(tri3, x):
    return _dot(tri3, jnp.concatenate(_split3(x), axis=0))


def _place_lanes(x, w3):
    return _dot(jnp.concatenate(_split3(x), axis=1), w3)


FFN_CHUNK = 512


def _ffn_ln_tile(x, wg_ref, wu_ref, wd_ref, g_ref, b_ref, alpha):
    xb = x.astype(BF16)
    ffn = wg_ref.shape[1]
    acc = jnp.zeros(x.shape, F32)
    for c0 in range(0, ffn, FFN_CHUNK):
        c1 = min(c0 + FFN_CHUNK, ffn)
        g = _dot(xb, wg_ref[:, c0:c1])
        u = _dot(xb, wu_ref[:, c0:c1])
        h = (g * jax.nn.sigmoid(g) * u).astype(BF16)
        acc = acc + _dot(h, wd_ref[c0:c1, :])
    return _layer_norm(alpha * x + 0.5 * acc, g_ref[...], b_ref[...])


_IN_XBC, _IN_UG, _IN_SMALL, _IN_QKV, _IN_Z = 1024, 512, LANES, 768, 512
_IN_RAW = _IN_XBC + _IN_UG + _IN_SMALL
FOX_SUB = 256
_TAIL = 8


def _causal_conv(buf_ref, x, w, b, first):
    ts = x.shape[0]
    buf_ref[0:_TAIL, :] = jnp.where(first, 0.0, buf_ref[0:_TAIL, :])
    buf_ref[_TAIL:_TAIL + ts, :] = x
    ext = buf_ref[...]
    y = b + w[CONV_K - 1:CONV_K, :] * x
    for j in range(CONV_K - 1):
        back = CONV_K - 1 - j
        y = y + w[j:j + 1, :] * pltpu.roll(ext, back, axis=0)[_TAIL:_TAIL + ts, :]
    buf_ref[0:_TAIL, :] = x[ts - _TAIL:ts, :]
    return y


def _lru_terms(u, wa_ref, ba_ref, wx_ref, bx_ref, lam_ref):
    ub = u.astype(BF16)
    r = jax.nn.sigmoid(_dot(ub, wa_ref[...]) + ba_ref[...])
    ig = jax.nn.sigmoid(_dot(ub, wx_ref[...]) + bx_ref[...])
    log_a = -LRU_C * r * jax.nn.softplus(-lam_ref[...])
    a = jnp.exp(log_a)
    b = jnp.sqrt(-jnp.tanh(log_a) * (a * a + 1.0)) * (ig * u)
    return a, b


def _fox_aug(sm, bf_ref, tri3_ref, place_ref, bias_ref, fcar_ref, aq_ref, ak_ref, first):
    pairs = aq_ref.shape[0]
    carry = jnp.where(first, 0.0, fcar_ref[0:1, :])
    for r in range(sm.shape[0] // FOX_SUB):
        rows = slice(r * FOX_SUB, (r + 1) * FOX_SUB)
        log_f = jax.nn.log_sigmoid(sm[rows, :] + bf_ref[...])
        fcum = _cumsum_rows(tri3_ref[...], log_f) + carry
        carry = fcum[FOX_SUB - 1:FOX_SUB, :]
        aug = (_place_lanes(fcum * LOG2E, place_ref[...]) + bias_ref[...]).astype(BF16)
        for pr in range(pairs):
            aq_ref[pr, rows, :] = aug[:, pr * LANES:(pr + 1) * LANES]
            ak_ref[pr, rows, :] = aug[:, (pairs + pr) * LANES:(pairs + pr + 1) * LANES]
    fcar_ref[...] = jnp.broadcast_to(carry, fcar_ref.shape)


def _pre_kernel(x_ref, wg_ref, wu_ref, wd_ref, g_ref, b_ref, w_ref,
                scw_ref, scb_ref, lcw_ref, lcb_ref, wa_ref, ba_ref, wx_ref, bx_ref, lam_ref, bf_ref,
                tri3_ref, place_ref, bias_ref,
                xo_ref, xs_ref, la_ref, lb_ref, lg_ref, aq_ref, ak_ref,
                q_ref, k_ref, ve_ref, vo_ref, z_ref, sm_ref,
                sbuf_ref, lbuf_ref, fcar_ref, raw_ref, *, alpha, tiles_per_seq):
    i = pl.program_id(0)
    lw = _IN_UG // 2

    @pl.when(i == 0)
    def _():
        raw_ref[...] = jnp.zeros(raw_ref.shape, F32)
        sbuf_ref[0:_TAIL, :] = jnp.zeros((_TAIL, sbuf_ref.shape[1]), F32)
        lbuf_ref[0:_TAIL, :] = jnp.zeros((_TAIL, lbuf_ref.shape[1]), F32)
        fcar_ref[...] = jnp.zeros(fcar_ref.shape, F32)

    first = (i + tiles_per_seq - 1) % tiles_per_seq == 0
    conv = _causal_conv(sbuf_ref, raw_ref[:, 0:_IN_XBC], scw_ref[...], scb_ref[...], first)
    xs_ref[...] = conv * jax.nn.sigmoid(conv)
    u = _causal_conv(lbuf_ref, raw_ref[:, _IN_XBC:_IN_XBC + lw], lcw_ref[...], lcb_ref[...], first)
    a, b = _lru_terms(u, wa_ref, ba_ref, wx_ref, bx_ref, lam_ref)
    la_ref[...] = a
    lb_ref[...] = b
    lg_ref[...] = jax.nn.gelu(raw_ref[:, _IN_XBC + lw:_IN_XBC + 2 * lw])
    _fox_aug(raw_ref[:, _IN_XBC + _IN_UG:_IN_RAW], bf_ref, tri3_ref, place_ref, bias_ref, fcar_ref,
             aq_ref, ak_ref, first)

    x1 = _ffn_ln_tile(x_ref[...], wg_ref, wu_ref, wd_ref, g_ref, b_ref, alpha)
    xo_ref[...] = x1
    hh = _dot(x1.astype(BF16), w_ref[...])
    raw_ref[...] = hh[:, 0:_IN_RAW]
    sm_ref[...] = hh[:, _IN_XBC + _IN_UG:_IN_RAW]
    h = hh[:, _IN_RAW:]
    o = 0
    aw = _IN_QKV // 3
    q = h[:, o:o + aw]
    k = h[:, o + aw:o + 2 * aw]
    v = h[:, o + 2 * aw:o + 3 * aw]
    o += _IN_QKV
    q_ref[...] = (q * (HEAD_DIM ** -0.5 * LOG2E)).astype(BF16)
    k_ref[...] = k.astype(BF16)
    even = (lax.broadcasted_iota(jnp.int32, v.shape, 1) % LANES) < HEAD_DIM
    ve_ref[...] = jnp.where(even, v, 1.0).astype(BF16)
    vo_ref[...] = jnp.where(even, 1.0, v).astype(BF16)
    z_ref[...] = h[:, o:o + _IN_Z]


def _pre(x, params, *, layer, alpha, seq, tm=512):
    t, d = x.shape
    aw = _IN_QKV // 3
    lw = _IN_UG // 2
    pairs = ATT_HEADS // 2
    place, bias = _fox_aug_consts()
    consts = [_tri3(FOX_SUB), place, bias]
    nt = t // tm

    def cur(n):
        return pl.BlockSpec((tm, n), lambda i: (jnp.minimum(i, nt - 1), 0))

    def prev(n):
        return pl.BlockSpec((tm, n), lambda i: (jnp.maximum(i - 1, 0), 0))

    aug = pl.BlockSpec((pairs, tm, LANES), lambda i: (0, jnp.maximum(i - 1, 0), 0))
    aug_shape = jax.ShapeDtypeStruct((pairs, t, LANES), BF16)
    outs = [(d, F32, cur), (_IN_XBC, F32, prev), (lw, F32, prev), (lw, F32, prev), (lw, F32, prev),
            None, None,
            (aw, BF16, cur), (aw, BF16, cur), (aw, BF16, cur), (aw, BF16, cur),
            (_IN_Z, F32, cur), (_IN_SMALL, F32, cur)]
    return pl.pallas_call(
        functools.partial(_pre_kernel, alpha=alpha, tiles_per_seq=seq // tm),
        out_shape=[aug_shape if o is None else jax.ShapeDtypeStruct((t, o[0]), o[1]) for o in outs],
        grid=(nt + 1,),
        in_specs=[cur(d)] + [_param(c, layer) for c in params] + [_resident(c.shape) for c in consts],
        out_specs=[aug if o is None else o[2](o[0]) for o in outs],
        scratch_shapes=[pltpu.VMEM((_TAIL + tm, _IN_XBC), F32), pltpu.VMEM((_TAIL + tm, lw), F32),
                        pltpu.VMEM((8, LANES), F32), pltpu.VMEM((tm, _IN_RAW), F32)],
        compiler_params=_cparams("arbitrary"),
        name="pre",
    )(x, *params, *consts)


def _tri3(n):
    tri = np.tril(np.ones((n, n), np.float32))
    return jnp.asarray(np.concatenate([tri, tri, tri], axis=1), BF16)


def _neg_mask(n):
    return jnp.asarray(np.where(np.tril(np.ones((n, n), bool)), 0.0, -np.inf), F32)


LRU_SUB = 64


def _lru_kernel(a_ref, b_ref, g_ref, o_ref, h_ref):
    first = pl.program_id(1) == 0
    ts, w = a_ref.shape

    @pl.when(first)
    def _():
        h_ref[...] = jnp.zeros(h_ref.shape, F32)

    carry = h_ref[0:1, :]
    row = lax.broadcasted_iota(jnp.int32, (LRU_SUB, w), 0)
    for r0 in range(0, ts, LRU_SUB):
        rows = slice(r0, r0 + LRU_SUB)
        a = a_ref[rows, :]
        b = b_ref[rows, :]
        d = 1
        while d < LRU_SUB:
            keep = row >= d
            a_s = jnp.where(keep, pltpu.roll(a, d, axis=0), 1.0)
            b_s = jnp.where(keep, pltpu.roll(b, d, axis=0), 0.0)
            b = a * b_s + b
            a = a * a_s
            d *= 2
        h = b + a * carry
        carry = h[LRU_SUB - 1:LRU_SUB, :]
        o_ref[rows, :] = (h * g_ref[rows, :]).astype(o_ref.dtype)
    h_ref[...] = jnp.broadcast_to(carry, h_ref.shape)


def _lru(a, b, g, *, batch, ts=1024):
    t, w = a.shape
    ns = t // batch // ts
    blk = pl.BlockSpec((ts, w), lambda bi, s: (bi * ns + s, 0))
    return pl.pallas_call(
        _lru_kernel,
        out_shape=jax.ShapeDtypeStruct((t, w), BF16),
        grid=(batch, ns),
        in_specs=[blk, blk, blk],
        out_specs=blk,
        scratch_shapes=[pltpu.VMEM((8, w), F32)],
        compiler_params=_cparams("arbitrary", "arbitrary"),
        name="lru",
    )(a, b, g)


def _fox_aug_consts():
    pairs = ATT_HEADS // 2
    place = np.zeros((3, LANES, 2 * pairs * LANES), np.float32)
    bias = np.zeros((1, 2 * pairs * LANES), np.float32)
    for h in range(ATT_HEADS):
        pr, e = divmod(h, 2)
        qcol = pr * LANES + AUG_STRIDE * e
        kcol = (pairs + pr) * LANES + AUG_STRIDE * e
        for term in range(3):
            place[term, h, qcol + term] = 1.0
            place[term, h, kcol + 3 + term] = -1.0
        bias[0, qcol + 3:qcol + 6] = 1.0
        bias[0, kcol:kcol + 3] = 1.0
    return jnp.asarray(place.reshape(3 * LANES, -1), BF16), jnp.asarray(bias, F32)


def _fox_kernel(q_ref, aq_ref, k_ref, ak_ref, ve_ref, vo_ref, nm_ref, o_ref,
                acc_ref, m_ref, s_ref, p_ref, al_ref, mx_ref, *, tq):
    tk = tq
    qi = pl.program_id(2)
    q2 = q_ref[...]
    aq = aq_ref[...]
    lane = lax.broadcasted_iota(jnp.int32, (tq, LANES), 1)
    low = lane < HEAD_DIM
    zero = jnp.zeros_like(q2)
    qs = []
    for e in range(2):
        own = jnp.where(low, q2, zero) if e == 0 else jnp.where(low, zero, q2)
        aug = jnp.where((lane >= AUG_STRIDE * e) & (lane < AUG_STRIDE * (e + 1)), aq, zero)
        qs.append(jnp.concatenate([own, aug], axis=1))
    v_refs = (ve_ref, vo_ref)
    reps = tk // LANES

    def rows_of(n):
        return pl.ds(pl.multiple_of(n * tk, tk), tk)

    def keys(n):
        return jnp.concatenate([k_ref[rows_of(n), :], ak_ref[rows_of(n), :]], axis=1)

    def scores(h, kb):
        return lax.dot_general(qs[h], kb, (((1,), (1,)), ((), ())), preferred_element_type=F32)

    def accumulate(h, slot, n):
        acc_ref[h] = al_ref[slot, h] * acc_ref[h] + _dot(p_ref[slot, h], v_refs[h][rows_of(n), :])

    def produce(slot, h, kb):
        s = scores(h, kb)
        s_ref[slot, h] = s
        mx_ref[slot, h] = jnp.broadcast_to(jnp.max(s, axis=1, keepdims=True), (tq, LANES))

    def stage(cur, n, mask_ref=None, lookahead=True):
        prv = 1 - cur
        if lookahead:
            kb = keys(n + 1)
        for h in range(2):
            s = s_ref[cur, h]
            m_prev = m_ref[h]
            if mask_ref is None:
                m_new = jnp.maximum(m_prev, mx_ref[cur, h])
            else:
                s = s + mask_ref[...]
                m_new = jnp.maximum(m_prev, jnp.max(s, axis=1, keepdims=True))
            al_ref[cur, h] = jnp.exp2(m_prev - m_new)
            m_ref[h] = m_new
            p_ref[cur, h] = jnp.exp2(s - jnp.tile(m_new, (1, reps))).astype(BF16)
            if lookahead:
                produce(prv, h, kb)
            accumulate(h, prv, jnp.maximum(n - 1, 0))

    acc_ref[...] = jnp.zeros(acc_ref.shape, F32)
    m_ref[...] = jnp.full(m_ref.shape, -jnp.inf, F32)
    p_ref[1] = jnp.zeros(p_ref.shape[1:], BF16)
    al_ref[1] = jnp.zeros(al_ref.shape[1:], F32)
    kb = keys(0)
    for h in range(2):
        produce(0, h, kb)

    def body(j, carry):
        stage(0, 2 * j)
        stage(1, 2 * j + 1)
        return carry

    lax.fori_loop(0, qi // 2, body, 0)

    @pl.when(qi % 2 == 0)
    def _():
        stage(0, qi, mask_ref=nm_ref, lookahead=False)
        for h in range(2):
            accumulate(h, 0, qi)

    @pl.when(qi % 2 == 1)
    def _():
        stage(0, qi - 1)
        stage(1, qi, mask_ref=nm_ref, lookahead=False)
        for h in range(2):
            accumulate(h, 1, qi)

    num = jnp.where(low, acc_ref[0], acc_ref[1])
    den = pltpu.roll(jnp.where(low, acc_ref[1], acc_ref[0]), HEAD_DIM, axis=1)
    o_ref[...] = (num / den).astype(o_ref.dtype)


def _fox(q, k, ve, vo, aq, ak, *, batch, tq=512):
    t, aw = q.shape
    s = t // batch
    nq = s // tq
    pairs = aw // LANES
    nm = _neg_mask(tq)
    qblk = pl.BlockSpec((tq, LANES), lambda b, hp, i: (b * nq + i, hp))
    kv = pl.BlockSpec((s, LANES), lambda b, hp, i: (b, hp))
    return pl.pallas_call(
        functools.partial(_fox_kernel, tq=tq),
        out_shape=jax.ShapeDtypeStruct((t, aw), BF16),
        grid=(batch, pairs, nq),
        in_specs=[qblk,
                  pl.BlockSpec((None, tq, LANES), lambda b, hp, i: (hp, b * nq + i, 0)),
                  kv,
                  pl.BlockSpec((None, s, LANES), lambda b, hp, i: (hp, b, 0)),
                  kv, kv, _resident(nm.shape)],
        out_specs=qblk,
        scratch_shapes=[pltpu.VMEM((2, tq, LANES), F32), pltpu.VMEM((2, tq, LANES), F32),
                        pltpu.VMEM((2, 2, tq, tq), F32), pltpu.VMEM((2, 2, tq, tq), BF16),
                        pltpu.VMEM((2, 2, tq, LANES), F32), pltpu.VMEM((2, 2, tq, LANES), F32)],
        compiler_params=_cparams("parallel", "parallel", "arbitrary"),
        name="fox",
    )(q, aq, k, ak, ve, vo, nm)


def _ssd_kernel(xs_ref, z_ref, sm_ref, dtb_ref, alog_ref, dskip_ref, ng_ref,
                tri3_ref, nm_ref, exp3_ref, o_ref, dt_ref, a_ref, state_ref):
    first = pl.program_id(1) == 0
    ts = xs_ref.shape[0]
    width = z_ref.shape[1]
    gw = width // SSD_GROUPS
    L = SSD_CHUNK
    nst = SSD_GROUPS * SSD_STATE

    @pl.when(first)
    def _():
        state_ref[...] = jnp.zeros(state_ref.shape, F32)

    dt = jax.nn.softplus(sm_ref[...] + dtb_ref[...])
    dt_ref[...] = dt
    a_ref[...] = -jnp.exp(alog_ref[...]) * dt

    low = lax.broadcasted_iota(jnp.int32, (L, LANES), 1) < HEAD_DIM

    def chunk(c):
        rows = pl.ds(c * L, L)
        xs = xs_ref[rows, 0:width]
        bm = xs_ref[rows, width:width + nst].astype(BF16)
        cm = xs_ref[rows, width + nst:width + 2 * nst].astype(BF16)
        a_cum = _cumsum_rows(tri3_ref[...], a_ref[rows, :])
        a_cum_t = a_cum.T
        a_exp = _place_lanes(a_cum, exp3_ref[...])
        xdt = xs * _place_lanes(dt_ref[rows, :], exp3_ref[...])
        decay_in = jnp.exp(a_exp)
        a_last = a_exp[L - 1:L, :]
        xdec = (jnp.exp(a_last - a_exp) * xdt).astype(BF16)
        chunk_decay = jnp.exp(a_last)
        zero = jnp.zeros((L, LANES), F32)
        y_parts = []
        for g in range(SSD_GROUPS):
            bg = bm[:, g * SSD_STATE:(g + 1) * SSD_STATE]
            cg = cm[:, g * SSD_STATE:(g + 1) * SSD_STATE]
            gmat = lax.dot_general(cg, bg, (((1,), (1,)), ((), ())), preferred_element_type=F32)
            for pr in range(gw // LANES):
                col = g * gw + pr * LANES
                hsl = slice(col, col + LANES)
                hd = col // HEAD_DIM
                blk = a_exp[:, hsl]
                swp = pltpu.roll(blk, HEAD_DIM, axis=1)
                cols = (jnp.where(low, blk, swp), jnp.where(low, swp, blk))
                ms = []
                for e in range(2):
                    r = DT_LANE0 + hd + e
                    seg = cols[e] - a_cum_t[r:r + 1, :]
                    ms.append((gmat * jnp.exp(seg + nm_ref[...])).astype(BF16))
                xp = xdt[:, hsl]
                rhs = jnp.concatenate([jnp.where(low, xp, zero).astype(BF16),
                                       jnp.where(low, zero, xp).astype(BF16)], axis=0)
                y_diag = _dot(jnp.concatenate(ms, axis=1), rhs)
                st = state_ref[:, hsl]
                y_off = _dot(cg, st.astype(BF16)) * decay_in[:, hsl]
                new = lax.dot_general(bg, xdec[:, hsl], (((0,), (0,)), ((), ())),
                                      preferred_element_type=F32)
                state_ref[:, hsl] = chunk_decay[:, hsl] * st + new
                y_parts.append(y_diag + y_off)
        y = jnp.concatenate(y_parts, axis=1) + dskip_ref[...] * xs
        zc = z_ref[rows, :]
        y = y * (zc * jax.nn.sigmoid(zc))
        outs = []
        for g in range(SSD_GROUPS):
            yg = y[:, g * gw:(g + 1) * gw]
            ms_ = jnp.mean(yg * yg, axis=-1, keepdims=True)
            outs.append(yg * lax.rsqrt(ms_ + RMS_EPS))
        o_ref[rows, :] = (jnp.concatenate(outs, axis=1) * ng_ref[...]).astype(o_ref.dtype)

    for c in range(ts // L):
        chunk(c)


def _ssd(xs, z, small, params, *, layer, batch, ts=1024):
    t, cdim = xs.shape
    width = z.shape[1]
    ns = t // batch // ts
    tri3 = _tri3(SSD_CHUNK)
    nm = _neg_mask(SSD_CHUNK)
    src = np.arange(LANES)[:, None]
    expand = (src == DT_LANE0 + np.arange(width)[None, :] // HEAD_DIM).astype(np.float32)
    exp3 = jnp.asarray(np.concatenate([expand] * 3, axis=0), BF16)

    def row(n):
        return pl.BlockSpec((ts, n), lambda b, s: (b * ns + s, 0))

    consts = [tri3, nm, exp3]
    return pl.pallas_call(
        _ssd_kernel,
        out_shape=jax.ShapeDtypeStruct((t, width), BF16),
        grid=(batch, ns),
        in_specs=([row(cdim), row(width), row(LANES)] + [_param(c, layer) for c in params]
                  + [_resident(c.shape) for c in consts]),
        out_specs=row(width),
        scratch_shapes=[pltpu.VMEM((ts, LANES), F32), pltpu.VMEM((ts, LANES), F32),
                        pltpu.VMEM((SSD_STATE, width), F32)],
        compiler_params=_cparams("arbitrary", "arbitrary"),
        name="ssd",
    )(xs, z, small, *params, *consts)


def _post_kernel(x_ref, ya_ref, yb_ref, yc_ref, p_ref, wo_ref, g2_ref, b2_ref, wg_ref, wu_ref, wd_ref,
                 g3_ref, b3_ref, pgw_ref, pgb_ref, ppw_ref, o_ref, *, alpha):
    na, nb = ya_ref.shape[1], yb_ref.shape[1]
    mix = (_dot(ya_ref[...], wo_ref[0:na, :]) + _dot(yb_ref[...], wo_ref[na:na + nb, :])
           + _dot(yc_ref[...], wo_ref[na + nb:, :]))
    x2 = _layer_norm(alpha * x_ref[...] + mix, g2_ref[...], b2_ref[...])
    x3 = _ffn_ln_tile(x2, wg_ref, wu_ref, wd_ref, g3_ref, b3_ref, alpha)
    gate = jax.nn.sigmoid(_dot(x3.astype(BF16), pgw_ref[...]) + pgb_ref[...])
    o_ref[...] = x3 + gate * _dot(p_ref[...].astype(BF16), ppw_ref[...])


def _post(x, ya, yb, yc, p, params, *, layer, alpha, tm=512):
    t, d = x.shape

    def row(n):
        return pl.BlockSpec((tm, n), lambda i: (i, 0))

    return pl.pallas_call(
        functools.partial(_post_kernel, alpha=alpha),
        out_shape=jax.ShapeDtypeStruct((t, d), F32),
        grid=(t // tm,),
        in_specs=([row(d), row(ya.shape[1]), row(yb.shape[1]), row(yc.shape[1]),
                   pl.BlockSpec((None, tm, p.shape[2]), lambda i: (layer, i, 0))]
                  + [_param(c, layer) for c in params]),
        out_specs=row(d),
        compiler_params=_cparams("parallel"),
        name="post",
    )(x, ya, yb, yc, p, *params)


def _block_diag(w):
    d, h, i, j = w.shape
    eye = jnp.eye(h, dtype=w.dtype)
    return (eye[None, :, None, :, None] * w[:, :, :, None, :]).reshape(d, h * i, h * j)


def _rows(v):
    return v[:, None, :].astype(F32)


def _head_lanes(v, offset):
    d, h = v.shape
    return jnp.zeros((d, 1, LANES), F32).at[:, 0, offset:offset + h].set(v.astype(F32))


def kernel(x, p, ln1_g, ln1_b, ffn1_wg, ffn1_wu, ffn1_wd, w_in, lru_conv_w, lru_conv_b, lru_wa, lru_ba, lru_wx, lru_bx, lru_lambda, fox_bf, ssd_conv_w, ssd_conv_b, ssd_dt_bias, ssd_a_log, ssd_d, ssd_norm_g, w_out, ln2_g, ln2_b, ffn2_wg, ffn2_wu, ffn2_wd, ln3_g, ln3_b, pe_proj, pe_gate_w, pe_gate_b):
    batch, seq, d_model = x.shape
    depth = p.shape[0]
    t = batch * seq
    alpha = (2.0 * depth) ** 0.25
    lru_w = lru_conv_w.shape[-1]
    att_w = ATT_HEADS * HEAD_DIM
    ssd_w = SSD_HEADS * HEAD_DIM
    conv_dim = ssd_conv_w.shape[-1]

    sizes = (lru_w, lru_w, att_w, att_w, att_w, ATT_HEADS, ssd_w, conv_dim, SSD_HEADS)
    offs = [0]
    for s_ in sizes:
        offs.append(offs[-1] + s_)
    col = lambda j: w_in[:, :, offs[j]:offs[j + 1]]
    pad = jnp.zeros((depth, d_model, LANES - ATT_HEADS - SSD_HEADS), F32)
    w_cat = jnp.concatenate([col(7), col(0), col(1), col(5), col(8), pad,
                             col(2), col(3), col(4), col(6)], axis=2).astype(BF16)

    pre_params = [ffn1_wg.astype(BF16), ffn1_wu.astype(BF16), ffn1_wd.astype(BF16),
                  _rows(ln1_g), _rows(ln1_b), w_cat,
                  ssd_conv_w, _rows(ssd_conv_b), lru_conv_w, _rows(lru_conv_b),
                  _block_diag(lru_wa).astype(BF16), _rows(lru_ba),
                  _block_diag(lru_wx).astype(BF16), _rows(lru_bx), _rows(lru_lambda),
                  _head_lanes(fox_bf, 0)]
    ssd_params = [_head_lanes(ssd_dt_bias, DT_LANE0), _head_lanes(ssd_a_log, DT_LANE0),
                  _rows(jnp.repeat(ssd_d, HEAD_DIM, axis=1)), _rows(ssd_norm_g)]
    post_params = [w_out.astype(BF16), _rows(ln2_g), _rows(ln2_b),
                   ffn2_wg.astype(BF16), ffn2_wu.astype(BF16), ffn2_wd.astype(BF16),
                   _rows(ln3_g), _rows(ln3_b),
                   pe_gate_w.astype(BF16), _rows(pe_gate_b), pe_proj.astype(BF16)]
    pf = p.reshape(depth, t, p.shape[-1])

    xf = x.reshape(t, d_model)
    for i in range(depth):
        xf, xs, la, lb, lg, aq, ak, q, k, ve, vo, z, small = _pre(xf, pre_params, layer=i, alpha=alpha,
                                                                  seq=seq)
        y_a = _lru(la, lb, lg, batch=batch)
        y_b = _fox(q, k, ve, vo, aq, ak, batch=batch)
        y_c = _ssd(xs, z, small, ssd_params, layer=i, batch=batch)
        xf = _post(xf, y_a, y_b, y_c, pf, post_params, layer=i, alpha=alpha)
    return xf.reshape(batch, seq, d_model)
```

```python
import functools
import math

import numpy as np
import jax
import jax.numpy as jnp
from jax import lax
from jax.experimental import pallas as pl
from jax.experimental.pallas import tpu as pltpu

F32 = jnp.float32
BF16 = jnp.bfloat16

LANES = 128
HEAD_DIM = 64
LRU_C = 8.0
CONV_K = 4
SSD_CHUNK = 128
SSD_STATE = 128
SSD_HEADS = 8
SSD_GROUPS = 2
ATT_HEADS = 4
DT_LANE0 = ATT_HEADS
AUG_STRIDE = 8
LN_EPS = 1e-5
RMS_EPS = 1e-5
LOG2E = math.log2(math.e)
VMEM_LIMIT = 56 * 1024 * 1024


def _cparams(*sem):
    return pltpu.CompilerParams(dimension_semantics=sem, vmem_limit_bytes=VMEM_LIMIT)


def _resident(shape):
    nd = len(shape)
    return pl.BlockSpec(shape, lambda *_: (0,) * nd, pipeline_mode=pl.Buffered(1))


def _param(arr, layer):
    nd = arr.ndim
    return pl.BlockSpec((None,) + arr.shape[1:], lambda *_: (layer,) + (0,) * (nd - 1),
                        pipeline_mode=pl.Buffered(1))


def _layer_norm(y, g, b):
    mu = jnp.mean(y, axis=-1, keepdims=True)
    d = y - mu
    var = jnp.mean(d * d, axis=-1, keepdims=True)
    return d * lax.rsqrt(var + LN_EPS) * g + b


def _dot(a, b):
    return jnp.dot(a, b, preferred_element_type=F32)


def _split3(x):
    hi = x.astype(BF16)
    r = x - hi.astype(F32)
    mid = r.astype(BF16)
    lo = (r - mid.astype(F32)).astype(BF16)
    return hi, mid, lo


def _cumsum_rows(tri3, x):
    return _dot(tri3, jnp.concatenate(_split3(x), axis=0))


def _place_lanes(x, w3):
    return _dot(jnp.concatenate(_split3(x), axis=1), w3)


FFN_CHUNK = 512


def _ffn_ln_tile(x, wg_ref, wu_ref, wd_ref, g_ref, b_ref, alpha, side_work=()):
    xb = x.astype(BF16)
    ffn = wg_ref.shape[1]
    starts = list(range(0, ffn, FFN_CHUNK))
    acc = jnp.zeros(x.shape, F32)
    for n, c0 in enumerate(starts):
        c1 = min(c0 + FFN_CHUNK, ffn)
        g = _dot(xb, wg_ref[:, c0:c1])
        u = _dot(xb, wu_ref[:, c0:c1])
        h = (g * jax.nn.sigmoid(g) * u).astype(BF16)
        acc = acc + _dot(h, wd_ref[c0:c1, :])
        bits = pltpu.bitcast(g[0:8, 0:LANES], jnp.int32)
        anchor = lax.shift_right_logical(lax.shift_right_logical(bits, 16), 16).astype(F32)
        for j, job in enumerate(side_work):
            if j * len(starts) // len(side_work) == n:
                job(anchor)
    return _layer_norm(alpha * x + 0.5 * acc, g_ref[...], b_ref[...])


_IN_XBC, _IN_UG, _IN_SMALL, _IN_QKV, _IN_Z = 1024, 512, LANES, 768, 512
_IN_RAW = _IN_XBC + _IN_UG + _IN_SMALL
FOX_SUB = 256
_TAIL = 8


def _causal_conv(buf_ref, x, w, b, first):
    ts = x.shape[0]
    buf_ref[0:_TAIL, :] = jnp.where(first, 0.0, buf_ref[0:_TAIL, :])
    buf_ref[_TAIL:_TAIL + ts, :] = x
    ext = buf_ref[...]
    y = b + w[CONV_K - 1:CONV_K, :] * x
    for j in range(CONV_K - 1):
        back = CONV_K - 1 - j
        y = y + w[j:j + 1, :] * pltpu.roll(ext, back, axis=0)[_TAIL:_TAIL + ts, :]
    buf_ref[0:_TAIL, :] = x[ts - _TAIL:ts, :]
    return y


def _lru_terms(u, wa_ref, ba_ref, wx_ref, bx_ref, lam_ref):
    ub = u.astype(BF16)
    r = jax.nn.sigmoid(_dot(ub, wa_ref[...]) + ba_ref[...])
    ig = jax.nn.sigmoid(_dot(ub, wx_ref[...]) + bx_ref[...])
    log_a = -LRU_C * r * jax.nn.softplus(-lam_ref[...])
    a = jnp.exp(log_a)
    b = jnp.sqrt(-jnp.tanh(log_a) * (a * a + 1.0)) * (ig * u)
    return a, b


def _fox_aug(sm, bf_ref, tri3_ref, place_ref, bias_ref, fcar_ref, aq_ref, ak_ref, first):
    pairs = aq_ref.shape[0]
    carry = jnp.where(first, 0.0, fcar_ref[0:1, :])
    for r in range(sm.shape[0] // FOX_SUB):
        rows = slice(r * FOX_SUB, (r + 1) * FOX_SUB)
        log_f = jax.nn.log_sigmoid(sm[rows, :] + bf_ref[...])
        fcum = _cumsum_rows(tri3_ref[...], log_f) + carry
        carry = fcum[FOX_SUB - 1:FOX_SUB, :]
        aug = (_place_lanes(fcum * LOG2E, place_ref[...]) + bias_ref[...]).astype(BF16)
        for pr in range(pairs):
            aq_ref[pr, rows, :] = aug[:, pr * LANES:(pr + 1) * LANES]
            ak_ref[pr, rows, :] = aug[:, (pairs + pr) * LANES:(pairs + pr + 1) * LANES]
    fcar_ref[...] = jnp.broadcast_to(carry, fcar_ref.shape)


def _pre_kernel(x_ref, wg_ref, wu_ref, wd_ref, g_ref, b_ref, w_ref,
                scw_ref, scb_ref, lcw_ref, lcb_ref, wa_ref, ba_ref, wx_ref, bx_ref, lam_ref, bf_ref,
                tri3_ref, place_ref, bias_ref,
                xo_ref, xs_ref, la_ref, lb_ref, lg_ref, aq_ref, ak_ref,
                q_ref, k_ref, ve_ref, vo_ref, z_ref, sm_ref,
                sbuf_ref, lbuf_ref, fcar_ref, raw_ref, *, alpha, tiles_per_seq):
    i = pl.program_id(0)
    lw = _IN_UG // 2

    @pl.when(i == 0)
    def _():
        raw_ref[...] = jnp.zeros(raw_ref.shape, F32)
        sbuf_ref[0:_TAIL, :] = jnp.zeros((_TAIL, sbuf_ref.shape[1]), F32)
        lbuf_ref[0:_TAIL, :] = jnp.zeros((_TAIL, lbuf_ref.shape[1]), F32)
        fcar_ref[...] = jnp.zeros(fcar_ref.shape, F32)

    first = (i + tiles_per_seq - 1) % tiles_per_seq == 0
    tm = raw_ref.shape[0]

    def raw(cols, anchor):
        return raw_ref[:, cols] + jnp.tile(anchor, (tm // 8, (cols.stop - cols.start) // LANES))

    def ssd_front(j):
        cols = slice(j * LANES, (j + 1) * LANES)

        def job(anchor):
            conv = _causal_conv(sbuf_ref.at[:, cols], raw(cols, anchor), scw_ref[:, cols], scb_ref[:, cols], first)
            xs_ref[:, cols] = conv * jax.nn.sigmoid(conv)
        return job

    def lru_front(anchor):
        u = _causal_conv(lbuf_ref, raw(slice(_IN_XBC, _IN_XBC + lw), anchor), lcw_ref[...], lcb_ref[...], first)
        a, b = _lru_terms(u, wa_ref, ba_ref, wx_ref, bx_ref, lam_ref)
        la_ref[...] = a
        lb_ref[...] = b
        lg_ref[...] = jax.nn.gelu(raw_ref[:, _IN_XBC + lw:_IN_XBC + 2 * lw])

    def fox_front(anchor):
        _fox_aug(raw(slice(_IN_XBC + _IN_UG, _IN_RAW), anchor), bf_ref, tri3_ref, place_ref, bias_ref,
                 fcar_ref, aq_ref, ak_ref, first)

    jobs = [ssd_front(j) for j in range(_IN_XBC // LANES)] + [lru_front, fox_front]

    x1 = _ffn_ln_tile(x_ref[...], wg_ref, wu_ref, wd_ref, g_ref, b_ref, alpha, side_work=jobs)
    xo_ref[...] = x1
    hh = _dot(x1.astype(BF16), w_ref[...])
    raw_ref[...] = hh[:, 0:_IN_RAW]
    sm_ref[...] = hh[:, _IN_XBC + _IN_UG:_IN_RAW]
    h = hh[:, _IN_RAW:]
    o = 0
    aw = _IN_QKV // 3
    q = h[:, o:o + aw]
    k = h[:, o + aw:o + 2 * aw]
    v = h[:, o + 2 * aw:o + 3 * aw]
    o += _IN_QKV
    q_ref[...] = (q * (HEAD_DIM ** -0.5 * LOG2E)).astype(BF16)
    k_ref[...] = k.astype(BF16)
    even = (lax.broadcasted_iota(jnp.int32, v.shape, 1) % LANES) < HEAD_DIM
    ve_ref[...] = jnp.where(even, v, 1.0).astype(BF16)
    vo_ref[...] = jnp.where(even, 1.0, v).astype(BF16)
    z_ref[...] = h[:, o:o + _IN_Z]


def _pre(x, params, *, layer, alpha, seq, tm=512):
    t, d = x.shape
    aw = _IN_QKV // 3
    lw = _IN_UG // 2
    pairs = ATT_HEADS // 2
    place, bias = _fox_aug_consts()
    consts = [_tri3(FOX_SUB), place, bias]
    nt = t // tm

    def cur(n):
        return pl.BlockSpec((tm, n), lambda i: (jnp.minimum(i, nt - 1), 0))

    def prev(n):
        return pl.BlockSpec((tm, n), lambda i: (jnp.maximum(i - 1, 0), 0))

    aug = pl.BlockSpec((pairs, tm, LANES), lambda i: (0, jnp.maximum(i - 1, 0), 0))
    aug_shape = jax.ShapeDtypeStruct((pairs, t, LANES), BF16)
    outs = [(d, F32, cur), (_IN_XBC, F32, prev), (lw, F32, prev), (lw, F32, prev), (lw, F32, prev),
            None, None,
            (aw, BF16, cur), (aw, BF16, cur), (aw, BF16, cur), (aw, BF16, cur),
            (_IN_Z, F32, cur), (_IN_SMALL, F32, cur)]
    return pl.pallas_call(
        functools.partial(_pre_kernel, alpha=alpha, tiles_per_seq=seq // tm),
        out_shape=[aug_shape if o is None else jax.ShapeDtypeStruct((t, o[0]), o[1]) for o in outs],
        grid=(nt + 1,),
        in_specs=[cur(d)] + [_param(c, layer) for c in params] + [_resident(c.shape) for c in consts],
        out_specs=[aug if o is None else o[2](o[0]) for o in outs],
        scratch_shapes=[pltpu.VMEM((_TAIL + tm, _IN_XBC), F32), pltpu.VMEM((_TAIL + tm, lw), F32),
                        pltpu.VMEM((8, LANES), F32), pltpu.VMEM((tm, _IN_RAW), F32)],
        compiler_params=_cparams("arbitrary"),
        name="pre",
    )(x, *params, *consts)


def _tri3(n):
    tri = np.tril(np.ones((n, n), np.float32))
    return jnp.asarray(np.concatenate([tri, tri, tri], axis=1), BF16)


def _neg_mask(n):
    return jnp.asarray(np.where(np.tril(np.ones((n, n), bool)), 0.0, -np.inf), F32)


LRU_SUB = 64


def _lru_kernel(a_ref, b_ref, g_ref, o_ref, h_ref):
    first = pl.program_id(1) == 0
    ts, w = a_ref.shape

    @pl.when(first)
    def _():
        h_ref[...] = jnp.zeros(h_ref.shape, F32)

    carry = h_ref[0:1, :]
    row = lax.broadcasted_iota(jnp.int32, (LRU_SUB, w), 0)
    for r0 in range(0, ts, LRU_SUB):
        rows = slice(r0, r0 + LRU_SUB)
        a = a_ref[rows, :]
        b = b_ref[rows, :]
        d = 1
        while d < LRU_SUB:
            keep = row >= d
            a_s = jnp.where(keep, pltpu.roll(a, d, axis=0), 1.0)
            b_s = jnp.where(keep, pltpu.roll(b, d, axis=0), 0.0)
            b = a * b_s + b
            a = a * a_s
            d *= 2
        h = b + a * carry
        carry = h[LRU_SUB - 1:LRU_SUB, :]
        o_ref[rows, :] = (h * g_ref[rows, :]).astype(o_ref.dtype)
    h_ref[...] = jnp.broadcast_to(carry, h_ref.shape)


def _lru(a, b, g, *, batch, ts=1024):
    t, w = a.shape
    ns = t // batch // ts
    blk = pl.BlockSpec((ts, w), lambda bi, s: (bi * ns + s, 0))
    return pl.pallas_call(
        _lru_kernel,
        out_shape=jax.ShapeDtypeStruct((t, w), BF16),
        grid=(batch, ns),
        in_specs=[blk, blk, blk],
        out_specs=blk,
        scratch_shapes=[pltpu.VMEM((8, w), F32)],
        compiler_params=_cparams("arbitrary", "arbitrary"),
        name="lru",
    )(a, b, g)


def _fox_aug_consts():
    pairs = ATT_HEADS // 2
    place = np.zeros((3, LANES, 2 * pairs * LANES), np.float32)
    bias = np.zeros((1, 2 * pairs * LANES), np.float32)
    for h in range(ATT_HEADS):
        pr, e = divmod(h, 2)
        qcol = pr * LANES + AUG_STRIDE * e
        kcol = (pairs + pr) * LANES + AUG_STRIDE * e
        for term in range(3):
            place[term, h, qcol + term] = 1.0
            place[term, h, kcol + 3 + term] = -1.0
        bias[0, qcol + 3:qcol + 6] = 1.0
        bias[0, kcol:kcol + 3] = 1.0
    return jnp.asarray(place.reshape(3 * LANES, -1), BF16), jnp.asarray(bias, F32)


def _fox_kernel(q_ref, aq_ref, k_ref, ak_ref, ve_ref, vo_ref, nm_ref, o_ref,
                acc_ref, m_ref, s_ref, p_ref, al_ref, mx_ref, *, tq):
    tk = tq
    qi = pl.program_id(2)
    q2 = q_ref[...]
    aq = aq_ref[...]
    lane = lax.broadcasted_iota(jnp.int32, (tq, LANES), 1)
    low = lane < HEAD_DIM
    zero = jnp.zeros_like(q2)
    qs = []
    for e in range(2):
        own = jnp.where(low, q2, zero) if e == 0 else jnp.where(low, zero, q2)
        aug = jnp.where((lane >= AUG_STRIDE * e) & (lane < AUG_STRIDE * (e + 1)), aq, zero)
        qs.append(jnp.concatenate([own, aug], axis=1))
    v_refs = (ve_ref, vo_ref)
    reps = tk // LANES

    def rows_of(n):
        return pl.ds(pl.multiple_of(n * tk, tk), tk)

    def keys(n):
        return jnp.concatenate([k_ref[rows_of(n), :], ak_ref[rows_of(n), :]], axis=1)

    def scores(h, kb):
        return lax.dot_general(qs[h], kb, (((1,), (1,)), ((), ())), preferred_element_type=F32)

    def accumulate(h, slot, n):
        acc_ref[h] = al_ref[slot, h] * acc_ref[h] + _dot(p_ref[slot, h], v_refs[h][rows_of(n), :])

    def produce(slot, h, kb):
        s = scores(h, kb)
        s_ref[slot, h] = s
        mx_ref[slot, h] = jnp.broadcast_to(jnp.max(s, axis=1, keepdims=True), (tq, LANES))

    def stage(cur, n, diag=None, lookahead=True):
        prv = 1 - cur
        if lookahead:
            kb = keys(n + 1)
        for h in range(2):
            s = s_ref[cur, h]
            m_prev = m_ref[h]
            if diag is None:
                m_new = jnp.maximum(m_prev, mx_ref[cur, h])
            else:
                s = s + nm_ref[:, diag * tk:(diag + 1) * tk]
                m_new = jnp.maximum(m_prev, jnp.max(s, axis=1, keepdims=True))
            al_ref[cur, h] = jnp.exp2(m_prev - m_new)
            m_ref[h] = m_new
            p_ref[cur, h] = jnp.exp2(s - jnp.tile(m_new, (1, reps))).astype(BF16)
            if lookahead:
                produce(prv, h, kb)
            accumulate(h, prv, jnp.maximum(n - 1, 0))

    acc_ref[...] = jnp.zeros(acc_ref.shape, F32)
    m_ref[...] = jnp.full(m_ref.shape, -jnp.inf, F32)
    p_ref[1] = jnp.zeros(p_ref.shape[1:], BF16)
    al_ref[1] = jnp.zeros(al_ref.shape[1:], F32)
    kb = keys(0)
    for h in range(2):
        produce(0, h, kb)

    def body(j, carry):
        stage(0, 2 * j)
        stage(1, 2 * j + 1)
        return carry

    lax.fori_loop(0, qi // 2, body, 0)

    @pl.when(qi % 2 == 0)
    def _():
        stage(0, qi, diag=0, lookahead=False)
        for h in range(2):
            accumulate(h, 0, qi)

    @pl.when(qi % 2 == 1)
    def _():
        stage(0, qi - 1)
        stage(1, qi, diag=0, lookahead=False)
        for h in range(2):
            accumulate(h, 1, qi)

    num = jnp.where(low, acc_ref[0], acc_ref[1])
    den = pltpu.roll(jnp.where(low, acc_ref[1], acc_ref[0]), HEAD_DIM, axis=1)
    o_ref[...] = (num / den).astype(o_ref.dtype)


def _fox(q, k, ve, vo, aq, ak, *, batch, tq=512):
    t, aw = q.shape
    s = t // batch
    nq = s // tq
    pairs = aw // LANES
    nm = _neg_mask(tq)
    qblk = pl.BlockSpec((tq, LANES), lambda b, hp, i: (b * nq + i, hp))
    kv = pl.BlockSpec((s, LANES), lambda b, hp, i: (b, hp))
    return pl.pallas_call(
        functools.partial(_fox_kernel, tq=tq),
        out_shape=jax.ShapeDtypeStruct((t, aw), BF16),
        grid=(batch, pairs, nq),
        in_specs=[qblk,
                  pl.BlockSpec((None, tq, LANES), lambda b, hp, i: (hp, b * nq + i, 0)),
                  kv,
                  pl.BlockSpec((None, s, LANES), lambda b, hp, i: (hp, b, 0)),
                  kv, kv, _resident(nm.shape)],
        out_specs=qblk,
        scratch_shapes=[pltpu.VMEM((2, tq, LANES), F32), pltpu.VMEM((2, tq, LANES), F32),
                        pltpu.VMEM((2, 2, tq, tq), F32), pltpu.VMEM((2, 2, tq, tq), BF16),
                        pltpu.VMEM((2, 2, tq, LANES), F32), pltpu.VMEM((2, 2, tq, LANES), F32)],
        compiler_params=_cparams("parallel", "parallel", "arbitrary"),
        name="fox",
    )(q, aq, k, ak, ve, vo, nm)


def _ssd_kernel(xs_ref, z_ref, sm_ref, dtb_ref, alog_ref, dskip_ref, ng_ref,
                tri3_ref, nm_ref, exp3_ref, o_ref, dt_ref, a_ref, state_ref):
    first = pl.program_id(1) == 0
    ts = xs_ref.shape[0]
    width = z_ref.shape[1]
    gw = width // SSD_GROUPS
    L = SSD_CHUNK
    nst = SSD_GROUPS * SSD_STATE

    @pl.when(first)
    def _():
        state_ref[...] = jnp.zeros(state_ref.shape, F32)

    dt = jax.nn.softplus(sm_ref[...] + dtb_ref[...])
    dt_ref[...] = dt
    a_ref[...] = -jnp.exp(alog_ref[...]) * dt

    low = lax.broadcasted_iota(jnp.int32, (L, LANES), 1) < HEAD_DIM

    def chunk(c):
        rows = pl.ds(c * L, L)
        xs = xs_ref[rows, 0:width]
        bm = xs_ref[rows, width:width + nst].astype(BF16)
        cm = xs_ref[rows, width + nst:width + 2 * nst].astype(BF16)
        a_cum = _cumsum_rows(tri3_ref[...], a_ref[rows, :])
        a_cum_t = a_cum.T
        a_exp = _place_lanes(a_cum, exp3_ref[...])
        xdt = xs * _place_lanes(dt_ref[rows, :], exp3_ref[...])
        decay_in = jnp.exp(a_exp)
        a_last = a_exp[L - 1:L, :]
        xdec = (jnp.exp(a_last - a_exp) * xdt).astype(BF16)
        chunk_decay = jnp.exp(a_last)
        zero = jnp.zeros((L, LANES), F32)
        y_parts = []
        for g in range(SSD_GROUPS):
            bg = bm[:, g * SSD_STATE:(g + 1) * SSD_STATE]
            cg = cm[:, g * SSD_STATE:(g + 1) * SSD_STATE]
            gmat = lax.dot_general(cg, bg, (((1,), (1,)), ((), ())), preferred_element_type=F32)
            for pr in range(gw // LANES):
                col = g * gw + pr * LANES
                hsl = slice(col, col + LANES)
                hd = col // HEAD_DIM
                blk = a_exp[:, hsl]
                swp = pltpu.roll(blk, HEAD_DIM, axis=1)
                cols = (jnp.where(low, blk, swp), jnp.where(low, swp, blk))
                ms = []
                for e in range(2):
                    r = DT_LANE0 + hd + e
                    seg = cols[e] - a_cum_t[r:r + 1, :]
                    ms.append((gmat * jnp.exp(seg + nm_ref[...])).astype(BF16))
                xp = xdt[:, hsl]
                rhs = jnp.concatenate([jnp.where(low, xp, zero).astype(BF16),
                                       jnp.where(low, zero, xp).astype(BF16)], axis=0)
                y_diag = _dot(jnp.concatenate(ms, axis=1), rhs)
                st = state_ref[:, hsl]
                y_off = _dot(cg, st.astype(BF16)) * decay_in[:, hsl]
                new = lax.dot_general(bg, xdec[:, hsl], (((0,), (0,)), ((), ())),
                                      preferred_element_type=F32)
                state_ref[:, hsl] = chunk_decay[:, hsl] * st + new
                y_parts.append(y_diag + y_off)
        y = jnp.concatenate(y_parts, axis=1) + dskip_ref[...] * xs
        zc = z_ref[rows, :]
        y = y * (zc * jax.nn.sigmoid(zc))
        outs = []
        for g in range(SSD_GROUPS):
            yg = y[:, g * gw:(g + 1) * gw]
            ms_ = jnp.mean(yg * yg, axis=-1, keepdims=True)
            outs.append(yg * lax.rsqrt(ms_ + RMS_EPS))
        o_ref[rows, :] = (jnp.concatenate(outs, axis=1) * ng_ref[...]).astype(o_ref.dtype)

    for c in range(ts // L):
        chunk(c)


def _ssd(xs, z, small, params, *, layer, batch, ts=1024):
    t, cdim = xs.shape
    width = z.shape[1]
    ns = t // batch // ts
    tri3 = _tri3(SSD_CHUNK)
    nm = _neg_mask(SSD_CHUNK)
    src = np.arange(LANES)[:, None]
    expand = (src == DT_LANE0 + np.arange(width)[None, :] // HEAD_DIM).astype(np.float32)
    exp3 = jnp.asarray(np.concatenate([expand] * 3, axis=0), BF16)

    def row(n):
        return pl.BlockSpec((ts, n), lambda b, s: (b * ns + s, 0))

    consts = [tri3, nm, exp3]
    return pl.pallas_call(
        _ssd_kernel,
        out_shape=jax.ShapeDtypeStruct((t, width), BF16),
        grid=(batch, ns),
        in_specs=([row(cdim), row(width), row(LANES)] + [_param(c, layer) for c in params]
                  + [_resident(c.shape) for c in consts]),
        out_specs=row(width),
        scratch_shapes=[pltpu.VMEM((ts, LANES), F32), pltpu.VMEM((ts, LANES), F32),
                        pltpu.VMEM((SSD_STATE, width), F32)],
        compiler_params=_cparams("arbitrary", "arbitrary"),
        name="ssd",
    )(xs, z, small, *params, *consts)


def _post_kernel(x_ref, ya_ref, yb_ref, yc_ref, p_ref, wo_ref, g2_ref, b2_ref, wg_ref, wu_ref, wd_ref,
                 g3_ref, b3_ref, pgw_ref, pgb_ref, ppw_ref, o_ref, *, alpha):
    na, nb = ya_ref.shape[1], yb_ref.shape[1]
    mix = (_dot(ya_ref[...], wo_ref[0:na, :]) + _dot(yb_ref[...], wo_ref[na:na + nb, :])
           + _dot(yc_ref[...], wo_ref[na + nb:, :]))
    x2 = _layer_norm(alpha * x_ref[...] + mix, g2_ref[...], b2_ref[...])
    x3 = _ffn_ln_tile(x2, wg_ref, wu_ref, wd_ref, g3_ref, b3_ref, alpha)
    gate = jax.nn.sigmoid(_dot(x3.astype(BF16), pgw_ref[...]) + pgb_ref[...])
    o_ref[...] = x3 + gate * _dot(p_ref[...].astype(BF16), ppw_ref[...])


def _post(x, ya, yb, yc, p, params, *, layer, alpha, tm=512):
    t, d = x.shape

    def row(n):
        return pl.BlockSpec((tm, n), lambda i: (i, 0))

    return pl.pallas_call(
        functools.partial(_post_kernel, alpha=alpha),
        out_shape=jax.ShapeDtypeStruct((t, d), F32),
        grid=(t // tm,),
        in_specs=([row(d), row(ya.shape[1]), row(yb.shape[1]), row(yc.shape[1]),
                   pl.BlockSpec((None, tm, p.shape[2]), lambda i: (layer, i, 0))]
                  + [_param(c, layer) for c in params]),
        out_specs=row(d),
        compiler_params=_cparams("parallel"),
        name="post",
    )(x, ya, yb, yc, p, *params)


def _block_diag(w):
    d, h, i, j = w.shape
    eye = jnp.eye(h, dtype=w.dtype)
    return (eye[None, :, None, :, None] * w[:, :, :, None, :]).reshape(d, h * i, h * j)


def _rows(v):
    return v[:, None, :].astype(F32)


def _head_lanes(v, offset):
    d, h = v.shape
    return jnp.zeros((d, 1, LANES), F32).at[:, 0, offset:offset + h].set(v.astype(F32))


def kernel(x, p, ln1_g, ln1_b, ffn1_wg, ffn1_wu, ffn1_wd, w_in, lru_conv_w, lru_conv_b, lru_wa, lru_ba, lru_wx, lru_bx, lru_lambda, fox_bf, ssd_conv_w, ssd_conv_b, ssd_dt_bias, ssd_a_log, ssd_d, ssd_norm_g, w_out, ln2_g, ln2_b, ffn2_wg, ffn2_wu, ffn2_wd, ln3_g, ln3_b, pe_proj, pe_gate_w, pe_gate_b):
    batch, seq, d_model = x.shape
    depth = p.shape[0]
    t = batch * seq
    alpha = (2.0 * depth) ** 0.25
    lru_w = lru_conv_w.shape[-1]
    att_w = ATT_HEADS * HEAD_DIM
    ssd_w = SSD_HEADS * HEAD_DIM
    conv_dim = ssd_conv_w.shape[-1]

    sizes = (lru_w, lru_w, att_w, att_w, att_w, ATT_HEADS, ssd_w, conv_dim, SSD_HEADS)
    offs = [0]
    for s_ in sizes:
        offs.append(offs[-1] + s_)
    col = lambda j: w_in[:, :, offs[j]:offs[j + 1]]
    pad = jnp.zeros((depth, d_model, LANES - ATT_HEADS - SSD_HEADS), F32)
    w_cat = jnp.concatenate([col(7), col(0), col(1), col(5), col(8), pad,
                             col(2), col(3), col(4), col(6)], axis=2).astype(BF16)

    pre_params = [ffn1_wg.astype(BF16), ffn1_wu.astype(BF16), ffn1_wd.astype(BF16),
                  _rows(ln1_g), _rows(ln1_b), w_cat,
                  ssd_conv_w, _rows(ssd_conv_b), lru_conv_w, _rows(lru_conv_b),
                  _block_diag(lru_wa).astype(BF16), _rows(lru_ba),
                  _block_diag(lru_wx).astype(BF16), _rows(lru_bx), _rows(lru_lambda),
                  _head_lanes(fox_bf, 0)]
    ssd_params = [_head_lanes(ssd_dt_bias, DT_LANE0), _head_lanes(ssd_a_log, DT_LANE0),
                  _rows(jnp.repeat(ssd_d, HEAD_DIM, axis=1)), _rows(ssd_norm_g)]
    post_params = [w_out.astype(BF16), _rows(ln2_g), _rows(ln2_b),
                   ffn2_wg.astype(BF16), ffn2_wu.astype(BF16), ffn2_wd.astype(BF16),
                   _rows(ln3_g), _rows(ln3_b),
                   pe_gate_w.astype(BF16), _rows(pe_gate_b), pe_proj.astype(BF16)]
    pf = p.reshape(depth, t, p.shape[-1])

    xf = x.reshape(t, d_model)
    for i in range(depth):
        xf, xs, la, lb, lg, aq, ak, q, k, ve, vo, z, small = _pre(xf, pre_params, layer=i, alpha=alpha,
                                                                  seq=seq)
        y_a = _lru(la, lb, lg, batch=batch)
        y_b = _fox(q, k, ve, vo, aq, ak, batch=batch)
        y_c = _ssd(xs, z, small, ssd_params, layer=i, batch=batch)
        xf = _post(xf, y_a, y_b, y_c, pf, post_params, layer=i, alpha=alpha)
    return xf.reshape(batch, seq, d_model)
```

```python
import functools
import math

import numpy as np
import jax
import jax.numpy as jnp
from jax import lax
from jax.experimental import pallas as pl
from jax.experimental.pallas import tpu as pltpu

F32 = jnp.float32
BF16 = jnp.bfloat16

LANES = 128
HEAD_DIM = 64
LRU_C = 8.0
CONV_K = 4
SSD_CHUNK = 128
SSD_STATE = 128
SSD_HEADS = 8
SSD_GROUPS = 2
ATT_HEADS = 4
DT_LANE0 = ATT_HEADS
AUG_STRIDE = 8
LN_EPS = 1e-5
RMS_EPS = 1e-5
LOG2E = math.log2(math.e)
VMEM_LIMIT = 56 * 1024 * 1024


def _cparams(*sem):
    return pltpu.CompilerParams(dimension_semantics=sem, vmem_limit_bytes=VMEM_LIMIT)


def _resident(shape):
    nd = len(shape)
    return pl.BlockSpec(shape, lambda *_: (0,) * nd, pipeline_mode=pl.Buffered(1))


def _param(arr, layer):
    nd = arr.ndim
    return pl.BlockSpec((None,) + arr.shape[1:], lambda *_: (layer,) + (0,) * (nd - 1),
                        pipeline_mode=pl.Buffered(1))


def _layer_norm(y, g, b):
    mu = jnp.mean(y, axis=-1, keepdims=True)
    d = y - mu
    var = jnp.mean(d * d, axis=-1, keepdims=True)
    return d * lax.rsqrt(var + LN_EPS) * g + b


def _dot(a, b):
    return jnp.dot(a, b, preferred_element_type=F32)


def _split3(x):
    hi = x.astype(BF16)
    r = x - hi.astype(F32)
    mid = r.astype(BF16)
    lo = (r - mid.astype(F32)).astype(BF16)
    return hi, mid, lo


def _cumsum_rows(tri3, x):
    return _dot(tri3, jnp.concatenate(_split3(x), axis=0))


def _place_lanes(x, w3):
    return _dot(jnp.concatenate(_split3(x), axis=1), w3)


FFN_CHUNK = 512


def _ffn_ln_tile(x, wg_ref, wu_ref, wd_ref, g_ref, b_ref, alpha, side_work=()):
    xb = x.astype(BF16)
    ffn = wg_ref.shape[1]
    starts = list(range(0, ffn, FFN_CHUNK))
    acc = jnp.zeros(x.shape, F32)
    for n, c0 in enumerate(starts):
        c1 = min(c0 + FFN_CHUNK, ffn)
        g = _dot(xb, wg_ref[:, c0:c1])
        u = _dot(xb, wu_ref[:, c0:c1])
        h = (g * jax.nn.sigmoid(g) * u).astype(BF16)
        acc = acc + _dot(h, wd_ref[c0:c1, :])
        bits = pltpu.bitcast(g[0:8, 0:LANES], jnp.int32)
        anchor = lax.shift_right_logical(lax.shift_right_logical(bits, 16), 16).astype(F32)
        for j, job in enumerate(side_work):
            if j * len(starts) // len(side_work) == n:
                job(anchor)
    return _layer_norm(alpha * x + 0.5 * acc, g_ref[...], b_ref[...])


_IN_XBC, _IN_UG, _IN_SMALL, _IN_QKV, _IN_Z = 1024, 512, LANES, 768, 512
_IN_RAW = _IN_XBC + _IN_UG + _IN_SMALL
FOX_SUB = 256
_TAIL = 8


def _causal_conv(buf_ref, x, w, b, first):
    ts = x.shape[0]
    buf_ref[0:_TAIL, :] = jnp.where(first, 0.0, buf_ref[0:_TAIL, :])
    buf_ref[_TAIL:_TAIL + ts, :] = x
    ext = buf_ref[...]
    y = b + w[CONV_K - 1:CONV_K, :] * x
    for j in range(CONV_K - 1):
        back = CONV_K - 1 - j
        y = y + w[j:j + 1, :] * pltpu.roll(ext, back, axis=0)[_TAIL:_TAIL + ts, :]
    buf_ref[0:_TAIL, :] = x[ts - _TAIL:ts, :]
    return y


def _lru_terms(u, wa_ref, ba_ref, wx_ref, bx_ref, lam_ref):
    ub = u.astype(BF16)
    r = jax.nn.sigmoid(_dot(ub, wa_ref[...]) + ba_ref[...])
    ig = jax.nn.sigmoid(_dot(ub, wx_ref[...]) + bx_ref[...])
    log_a = -LRU_C * r * jax.nn.softplus(-lam_ref[...])
    a = jnp.exp(log_a)
    b = jnp.sqrt(-jnp.tanh(log_a) * (a * a + 1.0)) * (ig * u)
    return a, b


def _fox_aug(sm, bf_ref, tri3_ref, place_ref, bias_ref, fcar_ref, aq_ref, ak_ref, first):
    pairs = aq_ref.shape[0]
    carry = jnp.where(first, 0.0, fcar_ref[0:1, :])
    for r in range(sm.shape[0] // FOX_SUB):
        rows = slice(r * FOX_SUB, (r + 1) * FOX_SUB)
        log_f = jax.nn.log_sigmoid(sm[rows, :] + bf_ref[...])
        fcum = _cumsum_rows(tri3_ref[...], log_f) + carry
        carry = fcum[FOX_SUB - 1:FOX_SUB, :]
        aug = (_place_lanes(fcum * LOG2E, place_ref[...]) + bias_ref[...]).astype(BF16)
        for pr in range(pairs):
            aq_ref[pr, rows, :] = aug[:, pr * LANES:(pr + 1) * LANES]
            ak_ref[pr, rows, :] = aug[:, (pairs + pr) * LANES:(pairs + pr + 1) * LANES]
    fcar_ref[...] = jnp.broadcast_to(carry, fcar_ref.shape)


def _pre_kernel(x_ref, wg_ref, wu_ref, wd_ref, g_ref, b_ref, w_ref,
                scw_ref, scb_ref, lcw_ref, lcb_ref, wa_ref, ba_ref, wx_ref, bx_ref, lam_ref, bf_ref,
                tri3_ref, place_ref, bias_ref,
                xo_ref, xs_ref, la_ref, lb_ref, lg_ref, aq_ref, ak_ref,
                q_ref, k_ref, ve_ref, vo_ref, z_ref, sm_ref,
                sbuf_ref, lbuf_ref, fcar_ref, raw_ref, *, alpha, tiles_per_seq):
    i = pl.program_id(0)
    lw = _IN_UG // 2

    @pl.when(i == 0)
    def _():
        raw_ref[...] = jnp.zeros(raw_ref.shape, F32)
        sbuf_ref[0:_TAIL, :] = jnp.zeros((_TAIL, sbuf_ref.shape[1]), F32)
        lbuf_ref[0:_TAIL, :] = jnp.zeros((_TAIL, lbuf_ref.shape[1]), F32)
        fcar_ref[...] = jnp.zeros(fcar_ref.shape, F32)

    first = (i + tiles_per_seq - 1) % tiles_per_seq == 0
    tm = raw_ref.shape[0]

    def raw(cols, anchor):
        return raw_ref[:, cols] + jnp.tile(anchor, (tm // 8, (cols.stop - cols.start) // LANES))

    def ssd_front(j):
        cols = slice(j * LANES, (j + 1) * LANES)

        def job(anchor):
            conv = _causal_conv(sbuf_ref.at[:, cols], raw(cols, anchor), scw_ref[:, cols], scb_ref[:, cols], first)
            xs_ref[:, cols] = conv * jax.nn.sigmoid(conv)
        return job

    def lru_front(anchor):
        u = _causal_conv(lbuf_ref, raw(slice(_IN_XBC, _IN_XBC + lw), anchor), lcw_ref[...], lcb_ref[...], first)
        a, b = _lru_terms(u, wa_ref, ba_ref, wx_ref, bx_ref, lam_ref)
        la_ref[...] = a
        lb_ref[...] = b
        lg_ref[...] = jax.nn.gelu(raw_ref[:, _IN_XBC + lw:_IN_XBC + 2 * lw])

    def fox_front(anchor):
        _fox_aug(raw(slice(_IN_XBC + _IN_UG, _IN_RAW), anchor), bf_ref, tri3_ref, place_ref, bias_ref,
                 fcar_ref, aq_ref, ak_ref, first)

    jobs = [ssd_front(j) for j in range(_IN_XBC // LANES)] + [lru_front, fox_front]

    x1 = _ffn_ln_tile(x_ref[...], wg_ref, wu_ref, wd_ref, g_ref, b_ref, alpha, side_work=jobs)
    xo_ref[...] = x1
    hh = _dot(x1.astype(BF16), w_ref[...])
    raw_ref[...] = hh[:, 0:_IN_RAW]
    sm_ref[...] = hh[:, _IN_XBC + _IN_UG:_IN_RAW]
    h = hh[:, _IN_RAW:]
    o = 0
    aw = _IN_QKV // 3
    q = h[:, o:o + aw]
    k = h[:, o + aw:o + 2 * aw]
    v = h[:, o + 2 * aw:o + 3 * aw]
    o += _IN_QKV
    q_ref[...] = (q * (HEAD_DIM ** -0.5 * LOG2E)).astype(BF16)
    k_ref[...] = k.astype(BF16)
    even = (lax.broadcasted_iota(jnp.int32, v.shape, 1) % LANES) < HEAD_DIM
    ve_ref[...] = jnp.where(even, v, 1.0).astype(BF16)
    vo_ref[...] = jnp.where(even, 1.0, v).astype(BF16)
    z_ref[...] = h[:, o:o + _IN_Z]


def _pre(x, params, *, layer, alpha, seq, tm=512):
    t, d = x.shape
    aw = _IN_QKV // 3
    lw = _IN_UG // 2
    pairs = ATT_HEADS // 2
    place, bias = _fox_aug_consts()
    consts = [_tri3(FOX_SUB), place, bias]
    nt = t // tm

    def cur(n):
        return pl.BlockSpec((tm, n), lambda i: (jnp.minimum(i, nt - 1), 0))

    def prev(n):
        return pl.BlockSpec((tm, n), lambda i: (jnp.maximum(i - 1, 0), 0))

    aug = pl.BlockSpec((pairs, tm, LANES), lambda i: (0, jnp.maximum(i - 1, 0), 0))
    aug_shape = jax.ShapeDtypeStruct((pairs, t, LANES), BF16)
    outs = [(d, F32, cur), (_IN_XBC, F32, prev), (lw, F32, prev), (lw, F32, prev), (lw, F32, prev),
            None, None,
            (aw, BF16, cur), (aw, BF16, cur), (aw, BF16, cur), (aw, BF16, cur),
            (_IN_Z, F32, cur), (_IN_SMALL, F32, cur)]
    return pl.pallas_call(
        functools.partial(_pre_kernel, alpha=alpha, tiles_per_seq=seq // tm),
        out_shape=[aug_shape if o is None else jax.ShapeDtypeStruct((t, o[0]), o[1]) for o in outs],
        grid=(nt + 1,),
        in_specs=[cur(d)] + [_param(c, layer) for c in params] + [_resident(c.shape) for c in consts],
        out_specs=[aug if o is None else o[2](o[0]) for o in outs],
        scratch_shapes=[pltpu.VMEM((_TAIL + tm, _IN_XBC), F32), pltpu.VMEM((_TAIL + tm, lw), F32),
                        pltpu.VMEM((8, LANES), F32), pltpu.VMEM((tm, _IN_RAW), F32)],
        compiler_params=_cparams("arbitrary"),
        name="pre",
    )(x, *params, *consts)


def _tri3(n):
    tri = np.tril(np.ones((n, n), np.float32))
    return jnp.asarray(np.concatenate([tri, tri, tri], axis=1), BF16)


def _neg_mask(n):
    return jnp.asarray(np.where(np.tril(np.ones((n, n), bool)), 0.0, -np.inf), F32)


LRU_SUB = 64


def _lru_kernel(a_ref, b_ref, g_ref, o_ref, h_ref):
    first = pl.program_id(1) == 0
    ts, w = a_ref.shape

    @pl.when(first)
    def _():
        h_ref[...] = jnp.zeros(h_ref.shape, F32)

    carry = h_ref[0:1, :]
    row = lax.broadcasted_iota(jnp.int32, (LRU_SUB, w), 0)
    for r0 in range(0, ts, LRU_SUB):
        rows = slice(r0, r0 + LRU_SUB)
        a = a_ref[rows, :]
        b = b_ref[rows, :]
        d = 1
        while d < LRU_SUB:
            keep = row >= d
            a_s = jnp.where(keep, pltpu.roll(a, d, axis=0), 1.0)
            b_s = jnp.where(keep, pltpu.roll(b, d, axis=0), 0.0)
            b = a * b_s + b
            a = a * a_s
            d *= 2
        h = b + a * carry
        carry = h[LRU_SUB - 1:LRU_SUB, :]
        o_ref[rows, :] = (h * g_ref[rows, :]).astype(o_ref.dtype)
    h_ref[...] = jnp.broadcast_to(carry, h_ref.shape)


def _lru(a, b, g, *, batch, ts=1024):
    t, w = a.shape
    ns = t // batch // ts
    blk = pl.BlockSpec((ts, w), lambda bi, s: (bi * ns + s, 0))
    return pl.pallas_call(
        _lru_kernel,
        out_shape=jax.ShapeDtypeStruct((t, w), BF16),
        grid=(batch, ns),
        in_specs=[blk, blk, blk],
        out_specs=blk,
        scratch_shapes=[pltpu.VMEM((8, w), F32)],
        compiler_params=_cparams("arbitrary", "arbitrary"),
        name="lru",
    )(a, b, g)


def _fox_aug_consts():
    pairs = ATT_HEADS // 2
    place = np.zeros((3, LANES, 2 * pairs * LANES), np.float32)
    bias = np.zeros((1, 2 * pairs * LANES), np.float32)
    for h in range(ATT_HEADS):
        pr, e = divmod(h, 2)
        qcol = pr * LANES + AUG_STRIDE * e
        kcol = (pairs + pr) * LANES + AUG_STRIDE * e
        for term in range(3):
            place[term, h, qcol + term] = 1.0
            place[term, h, kcol + 3 + term] = -1.0
        bias[0, qcol + 3:qcol + 6] = 1.0
        bias[0, kcol:kcol + 3] = 1.0
    return jnp.asarray(place.reshape(3 * LANES, -1), BF16), jnp.asarray(bias, F32)


def _fox_kernel(q_ref, aq_ref, k_ref, ak_ref, ve_ref, vo_ref, nm_ref, o_ref,
                acc_ref, m_ref, s_ref, p_ref, al_ref, mx_ref, *, tq):
    tk = tq
    qi = pl.program_id(2)
    q2 = q_ref[...]
    aq = aq_ref[...]
    lane = lax.broadcasted_iota(jnp.int32, (tq, LANES), 1)
    low = lane < HEAD_DIM
    zero = jnp.zeros_like(q2)
    qs = []
    for e in range(2):
        own = jnp.where(low, q2, zero) if e == 0 else jnp.where(low, zero, q2)
        aug = jnp.where((lane >= AUG_STRIDE * e) & (lane < AUG_STRIDE * (e + 1)), aq, zero)
        qs.append(jnp.concatenate([own, aug], axis=1))
    v_refs = (ve_ref, vo_ref)
    reps = tk // LANES

    def rows_of(n):
        return pl.ds(pl.multiple_of(n * tk, tk), tk)

    def keys(n):
        return jnp.concatenate([k_ref[rows_of(n), :], ak_ref[rows_of(n), :]], axis=1)

    def scores(h, kb):
        return lax.dot_general(qs[h], kb, (((1,), (1,)), ((), ())), preferred_element_type=F32)

    def accumulate(h, slot, n):
        acc_ref[h] = al_ref[slot, h] * acc_ref[h] + _dot(p_ref[slot, h], v_refs[h][rows_of(n), :])

    def produce(slot, h, kb):
        s = scores(h, kb)
        s_ref[slot, h] = s
        mx_ref[slot, h] = jnp.broadcast_to(jnp.max(s, axis=1, keepdims=True), (tq, LANES))

    def stage(cur, n, diag=None, lookahead=True):
        prv = 1 - cur
        if lookahead:
            kb = keys(n + 1)
        for h in range(2):
            s = s_ref[cur, h]
            m_prev = m_ref[h]
            if diag is None:
                m_new = jnp.maximum(m_prev, mx_ref[cur, h])
            else:
                s = s + nm_ref[:, diag * tk:(diag + 1) * tk]
                m_new = jnp.maximum(m_prev, jnp.max(s, axis=1, keepdims=True))
            al_ref[cur, h] = jnp.exp2(m_prev - m_new)
            m_ref[h] = m_new
            p_ref[cur, h] = jnp.exp2(s - jnp.tile(m_new, (1, reps))).astype(BF16)
            if lookahead:
                produce(prv, h, kb)
            accumulate(h, prv, jnp.maximum(n - 1, 0))

    acc_ref[...] = jnp.zeros(acc_ref.shape, F32)
    m_ref[...] = jnp.full(m_ref.shape, -jnp.inf, F32)
    p_ref[1] = jnp.zeros(p_ref.shape[1:], BF16)
    al_ref[1] = jnp.zeros(al_ref.shape[1:], F32)
    kb = keys(0)
    for h in range(2):
        produce(0, h, kb)

    def body4(j, carry):
        for r in range(4):
            stage(r % 2, 4 * j + r)
        return carry

    def body2(j, carry):
        stage(0, 2 * j)
        stage(1, 2 * j + 1)
        return carry

    pairs_left = qi // 2
    lax.fori_loop(0, pairs_left // 2, body4, 0)
    lax.fori_loop(2 * (pairs_left // 2), pairs_left, body2, 0)

    @pl.when(qi % 2 == 0)
    def _():
        stage(0, qi, diag=0, lookahead=False)
        for h in range(2):
            accumulate(h, 0, qi)

    @pl.when(qi % 2 == 1)
    def _():
        stage(0, qi - 1)
        stage(1, qi, diag=0, lookahead=False)
        for h in range(2):
            accumulate(h, 1, qi)

    num = jnp.where(low, acc_ref[0], acc_ref[1])
    den = pltpu.roll(jnp.where(low, acc_ref[1], acc_ref[0]), HEAD_DIM, axis=1)
    o_ref[...] = (num / den).astype(o_ref.dtype)


def _fox(q, k, ve, vo, aq, ak, *, batch, tq=512):
    t, aw = q.shape
    s = t // batch
    nq = s // tq
    pairs = aw // LANES
    nm = _neg_mask(tq)
    qblk = pl.BlockSpec((tq, LANES), lambda b, hp, i: (b * nq + i, hp))
    kv = pl.BlockSpec((s, LANES), lambda b, hp, i: (b, hp))
    return pl.pallas_call(
        functools.partial(_fox_kernel, tq=tq),
        out_shape=jax.ShapeDtypeStruct((t, aw), BF16),
        grid=(batch, pairs, nq),
        in_specs=[qblk,
                  pl.BlockSpec((None, tq, LANES), lambda b, hp, i: (hp, b * nq + i, 0)),
                  kv,
                  pl.BlockSpec((None, s, LANES), lambda b, hp, i: (hp, b, 0)),
                  kv, kv, _resident(nm.shape)],
        out_specs=qblk,
        scratch_shapes=[pltpu.VMEM((2, tq, LANES), F32), pltpu.VMEM((2, tq, LANES), F32),
                        pltpu.VMEM((2, 2, tq, tq), F32), pltpu.VMEM((2, 2, tq, tq), BF16),
                        pltpu.VMEM((2, 2, tq, LANES), F32), pltpu.VMEM((2, 2, tq, LANES), F32)],
        compiler_params=_cparams("parallel", "parallel", "arbitrary"),
        name="fox",
    )(q, aq, k, ak, ve, vo, nm)


def _ssd_kernel(xs_ref, z_ref, sm_ref, dtb_ref, alog_ref, dskip_ref, ng_ref,
                tri3_ref, nm_ref, exp3_ref, o_ref, dt_ref, a_ref, state_ref):
    first = pl.program_id(1) == 0
    ts = xs_ref.shape[0]
    width = z_ref.shape[1]
    gw = width // SSD_GROUPS
    L = SSD_CHUNK
    nst = SSD_GROUPS * SSD_STATE

    @pl.when(first)
    def _():
        state_ref[...] = jnp.zeros(state_ref.shape, F32)

    dt = jax.nn.softplus(sm_ref[...] + dtb_ref[...])
    dt_ref[...] = dt
    a_ref[...] = -jnp.exp(alog_ref[...]) * dt

    low = lax.broadcasted_iota(jnp.int32, (L, LANES), 1) < HEAD_DIM

    def chunk(c):
        rows = pl.ds(c * L, L)
        xs = xs_ref[rows, 0:width]
        bm = xs_ref[rows, width:width + nst].astype(BF16)
        cm = xs_ref[rows, width + nst:width + 2 * nst].astype(BF16)
        a_cum = _cumsum_rows(tri3_ref[...], a_ref[rows, :])
        a_cum_t = a_cum.T
        a_exp = _place_lanes(a_cum, exp3_ref[...])
        xdt = xs * _place_lanes(dt_ref[rows, :], exp3_ref[...])
        decay_in = jnp.exp(a_exp)
        a_last = a_exp[L - 1:L, :]
        xdec = (jnp.exp(a_last - a_exp) * xdt).astype(BF16)
        chunk_decay = jnp.exp(a_last)
        zero = jnp.zeros((L, LANES), F32)
        y_parts = []
        for g in range(SSD_GROUPS):
            bg = bm[:, g * SSD_STATE:(g + 1) * SSD_STATE]
            cg = cm[:, g * SSD_STATE:(g + 1) * SSD_STATE]
            gmat = lax.dot_general(cg, bg, (((1,), (1,)), ((), ())), preferred_element_type=F32)
            for pr in range(gw // LANES):
                col = g * gw + pr * LANES
                hsl = slice(col, col + LANES)
                hd = col // HEAD_DIM
                blk = a_exp[:, hsl]
                swp = pltpu.roll(blk, HEAD_DIM, axis=1)
                cols = (jnp.where(low, blk, swp), jnp.where(low, swp, blk))
                ms = []
                for e in range(2):
                    r = DT_LANE0 + hd + e
                    seg = cols[e] - a_cum_t[r:r + 1, :]
                    ms.append((gmat * jnp.exp(seg + nm_ref[...])).astype(BF16))
                xp = xdt[:, hsl]
                rhs = jnp.concatenate([jnp.where(low, xp, zero).astype(BF16),
                                       jnp.where(low, zero, xp).astype(BF16)], axis=0)
                y_diag = _dot(jnp.concatenate(ms, axis=1), rhs)
                st = state_ref[:, hsl]
                y_off = _dot(cg, st.astype(BF16)) * decay_in[:, hsl]
                new = lax.dot_general(bg, xdec[:, hsl], (((0,), (0,)), ((), ())),
                                      preferred_element_type=F32)
                state_ref[:, hsl] = chunk_decay[:, hsl] * st + new
                y_parts.append(y_diag + y_off)
        y = jnp.concatenate(y_parts, axis=1) + dskip_ref[...] * xs
        zc = z_ref[rows, :]
        y = y * (zc * jax.nn.sigmoid(zc))
        outs = []
        for g in range(SSD_GROUPS):
            yg = y[:, g * gw:(g + 1) * gw]
            ms_ = jnp.mean(yg * yg, axis=-1, keepdims=True)
            outs.append(yg * lax.rsqrt(ms_ + RMS_EPS))
        o_ref[rows, :] = (jnp.concatenate(outs, axis=1) * ng_ref[...]).astype(o_ref.dtype)

    for c in range(ts // L):
        chunk(c)


def _ssd(xs, z, small, params, *, layer, batch, ts=1024):
    t, cdim = xs.shape
    width = z.shape[1]
    ns = t // batch // ts
    tri3 = _tri3(SSD_CHUNK)
    nm = _neg_mask(SSD_CHUNK)
    src = np.arange(LANES)[:, None]
    expand = (src == DT_LANE0 + np.arange(width)[None, :] // HEAD_DIM).astype(np.float32)
    exp3 = jnp.asarray(np.concatenate([expand] * 3, axis=0), BF16)

    def row(n):
        return pl.BlockSpec((ts, n), lambda b, s: (b * ns + s, 0))

    consts = [tri3, nm, exp3]
    return pl.pallas_call(
        _ssd_kernel,
        out_shape=jax.ShapeDtypeStruct((t, width), BF16),
        grid=(batch, ns),
        in_specs=([row(cdim), row(width), row(LANES)] + [_param(c, layer) for c in params]
                  + [_resident(c.shape) for c in consts]),
        out_specs=row(width),
        scratch_shapes=[pltpu.VMEM((ts, LANES), F32), pltpu.VMEM((ts, LANES), F32),
                        pltpu.VMEM((SSD_STATE, width), F32)],
        compiler_params=_cparams("arbitrary", "arbitrary"),
        name="ssd",
    )(xs, z, small, *params, *consts)


def _post_kernel(x_ref, ya_ref, yb_ref, yc_ref, p_ref, wo_ref, g2_ref, b2_ref, wg_ref, wu_ref, wd_ref,
                 g3_ref, b3_ref, pgw_ref, pgb_ref, ppw_ref, o_ref, *, alpha):
    na, nb = ya_ref.shape[1], yb_ref.shape[1]
    mix = (_dot(ya_ref[...], wo_ref[0:na, :]) + _dot(yb_ref[...], wo_ref[na:na + nb, :])
           + _dot(yc_ref[...], wo_ref[na + nb:, :]))
    x2 = _layer_norm(alpha * x_ref[...] + mix, g2_ref[...], b2_ref[...])
    x3 = _ffn_ln_tile(x2, wg_ref, wu_ref, wd_ref, g3_ref, b3_ref, alpha)
    gate = jax.nn.sigmoid(_dot(x3.astype(BF16), pgw_ref[...]) + pgb_ref[...])
    o_ref[...] = x3 + gate * _dot(p_ref[...].astype(BF16), ppw_ref[...])


def _post(x, ya, yb, yc, p, params, *, layer, alpha, tm=512):
    t, d = x.shape

    def row(n):
        return pl.BlockSpec((tm, n), lambda i: (i, 0))

    return pl.pallas_call(
        functools.partial(_post_kernel, alpha=alpha),
        out_shape=jax.ShapeDtypeStruct((t, d), F32),
        grid=(t // tm,),
        in_specs=([row(d), row(ya.shape[1]), row(yb.shape[1]), row(yc.shape[1]),
                   pl.BlockSpec((None, tm, p.shape[2]), lambda i: (layer, i, 0))]
                  + [_param(c, layer) for c in params]),
        out_specs=row(d),
        compiler_params=_cparams("parallel"),
        name="post",
    )(x, ya, yb, yc, p, *params)


def _block_diag(w):
    d, h, i, j = w.shape
    eye = jnp.eye(h, dtype=w.dtype)
    return (eye[None, :, None, :, None] * w[:, :, :, None, :]).reshape(d, h * i, h * j)


def _rows(v):
    return v[:, None, :].astype(F32)


def _head_lanes(v, offset):
    d, h = v.shape
    return jnp.zeros((d, 1, LANES), F32).at[:, 0, offset:offset + h].set(v.astype(F32))


def kernel(x, p, ln1_g, ln1_b, ffn1_wg, ffn1_wu, ffn1_wd, w_in, lru_conv_w, lru_conv_b, lru_wa, lru_ba, lru_wx, lru_bx, lru_lambda, fox_bf, ssd_conv_w, ssd_conv_b, ssd_dt_bias, ssd_a_log, ssd_d, ssd_norm_g, w_out, ln2_g, ln2_b, ffn2_wg, ffn2_wu, ffn2_wd, ln3_g, ln3_b, pe_proj, pe_gate_w, pe_gate_b):
    batch, seq, d_model = x.shape
    depth = p.shape[0]
    t = batch * seq
    alpha = (2.0 * depth) ** 0.25
    lru_w = lru_conv_w.shape[-1]
    att_w = ATT_HEADS * HEAD_DIM
    ssd_w = SSD_HEADS * HEAD_DIM
    conv_dim = ssd_conv_w.shape[-1]

    sizes = (lru_w, lru_w, att_w, att_w, att_w, ATT_HEADS, ssd_w, conv_dim, SSD_HEADS)
    offs = [0]
    for s_ in sizes:
        offs.append(offs[-1] + s_)
    col = lambda j: w_in[:, :, offs[j]:offs[j + 1]]
    pad = jnp.zeros((depth, d_model, LANES - ATT_HEADS - SSD_HEADS), F32)
    w_cat = jnp.concatenate([col(7), col(0), col(1), col(5), col(8), pad,
                             col(2), col(3), col(4), col(6)], axis=2).astype(BF16)

    pre_params = [ffn1_wg.astype(BF16), ffn1_wu.astype(BF16), ffn1_wd.astype(BF16),
                  _rows(ln1_g), _rows(ln1_b), w_cat,
                  ssd_conv_w, _rows(ssd_conv_b), lru_conv_w, _rows(lru_conv_b),
                  _block_diag(lru_wa).astype(BF16), _rows(lru_ba),
                  _block_diag(lru_wx).astype(BF16), _rows(lru_bx), _rows(lru_lambda),
                  _head_lanes(fox_bf, 0)]
    ssd_params = [_head_lanes(ssd_dt_bias, DT_LANE0), _head_lanes(ssd_a_log, DT_LANE0),
                  _rows(jnp.repeat(ssd_d, HEAD_DIM, axis=1)), _rows(ssd_norm_g)]
    post_params = [w_out.astype(BF16), _rows(ln2_g), _rows(ln2_b),
                   ffn2_wg.astype(BF16), ffn2_wu.astype(BF16), ffn2_wd.astype(BF16),
                   _rows(ln3_g), _rows(ln3_b),
                   pe_gate_w.astype(BF16), _rows(pe_gate_b), pe_proj.astype(BF16)]
    pf = p.reshape(depth, t, p.shape[-1])

    xf = x.reshape(t, d_model)
    for i in range(depth):
        xf, xs, la, lb, lg, aq, ak, q, k, ve, vo, z, small = _pre(xf, pre_params, layer=i, alpha=alpha,
                                                                  seq=seq)
        y_a = _lru(la, lb, lg, batch=batch)
        y_b = _fox(q, k, ve, vo, aq, ak, batch=batch)
        y_c = _ssd(xs, z, small, ssd_params, layer=i, batch=batch)
        xf = _post(xf, y_a, y_b, y_c, pf, post_params, layer=i, alpha=alpha)
    return xf.reshape(batch, seq, d_model)
```

```python
import functools
import math

import numpy as np
import jax
import jax.numpy as jnp
from jax import lax
from jax.experimental import pallas as pl
from jax.experimental.pallas import tpu as pltpu

F32 = jnp.float32
BF16 = jnp.bfloat16

LANES = 128
HEAD_DIM = 64
LRU_C = 8.0
CONV_K = 4
SSD_CHUNK = 128
SSD_STATE = 128
SSD_HEADS = 8
SSD_GROUPS = 2
ATT_HEADS = 4
DT_LANE0 = ATT_HEADS
AUG_STRIDE = 8
LN_EPS = 1e-5
RMS_EPS = 1e-5
LOG2E = math.log2(math.e)
VMEM_LIMIT = 56 * 1024 * 1024


def _cparams(*sem):
    return pltpu.CompilerParams(dimension_semantics=sem, vmem_limit_bytes=VMEM_LIMIT)


def _resident(shape):
    nd = len(shape)
    return pl.BlockSpec(shape, lambda *_: (0,) * nd, pipeline_mode=pl.Buffered(1))


def _param(arr, layer):
    nd = arr.ndim
    return pl.BlockSpec((None,) + arr.shape[1:], lambda *_: (layer,) + (0,) * (nd - 1),
                        pipeline_mode=pl.Buffered(1))


def _layer_norm(y, g, b):
    mu = jnp.mean(y, axis=-1, keepdims=True)
    d = y - mu
    var = jnp.mean(d * d, axis=-1, keepdims=True)
    return d * lax.rsqrt(var + LN_EPS) * g + b


def _dot(a, b):
    return jnp.dot(a, b, preferred_element_type=F32)


def _split3(x):
    hi = x.astype(BF16)
    r = x - hi.astype(F32)
    mid = r.astype(BF16)
    lo = (r - mid.astype(F32)).astype(BF16)
    return hi, mid, lo


def _cumsum_rows(tri3, x):
    return _dot(tri3, jnp.concatenate(_split3(x), axis=0))


def _place_lanes(x, w3):
    return _dot(jnp.concatenate(_split3(x), axis=1), w3)


FFN_CHUNK = 512


def _ffn_ln_tile(x, wg_ref, wu_ref, wd_ref, g_ref, b_ref, alpha, side_work=()):
    xb = x.astype(BF16)
    ffn = wg_ref.shape[1]
    starts = list(range(0, ffn, FFN_CHUNK))
    acc = jnp.zeros(x.shape, F32)
    for n, c0 in enumerate(starts):
        c1 = min(c0 + FFN_CHUNK, ffn)
        g = _dot(xb, wg_ref[:, c0:c1])
        u = _dot(xb, wu_ref[:, c0:c1])
        h = (g * jax.nn.sigmoid(g) * u).astype(BF16)
        acc = acc + _dot(h, wd_ref[c0:c1, :])
        bits = pltpu.bitcast(g[0:8, 0:LANES], jnp.int32)
        anchor = lax.shift_right_logical(lax.shift_right_logical(bits, 16), 16).astype(F32)
        for j, job in enumerate(side_work):
            if j * len(starts) // len(side_work) == n:
                job(anchor)
    return _layer_norm(alpha * x + 0.5 * acc, g_ref[...], b_ref[...])


_IN_XBC, _IN_UG, _IN_SMALL, _IN_QKV, _IN_Z = 1024, 512, LANES, 768, 512
_IN_RAW = _IN_XBC + _IN_UG + _IN_SMALL
FOX_SUB = 256
_TAIL = 8


def _causal_conv(buf_ref, x, w, b, first):
    ts = x.shape[0]
    buf_ref[0:_TAIL, :] = jnp.where(first, 0.0, buf_ref[0:_TAIL, :])
    buf_ref[_TAIL:_TAIL + ts, :] = x
    ext = buf_ref[...]
    y = b + w[CONV_K - 1:CONV_K, :] * x
    for j in range(CONV_K - 1):
        back = CONV_K - 1 - j
        y = y + w[j:j + 1, :] * pltpu.roll(ext, back, axis=0)[_TAIL:_TAIL + ts, :]
    buf_ref[0:_TAIL, :] = x[ts - _TAIL:ts, :]
    return y


def _lru_terms(u, wa_ref, ba_ref, wx_ref, bx_ref, lam_ref):
    ub = u.astype(BF16)
    r = jax.nn.sigmoid(_dot(ub, wa_ref[...]) + ba_ref[...])
    ig = jax.nn.sigmoid(_dot(ub, wx_ref[...]) + bx_ref[...])
    log_a = -LRU_C * r * jax.nn.softplus(-lam_ref[...])
    a = jnp.exp(log_a)
    b = jnp.sqrt(-jnp.tanh(log_a) * (a * a + 1.0)) * (ig * u)
    return a, b


def _fox_aug(sm, bf_ref, tri3_ref, place_ref, bias_ref, fcar_ref, aq_ref, ak_ref, first):
    pairs = aq_ref.shape[0]
    carry = jnp.where(first, 0.0, fcar_ref[0:1, :])
    for r in range(sm.shape[0] // FOX_SUB):
        rows = slice(r * FOX_SUB, (r + 1) * FOX_SUB)
        log_f = jax.nn.log_sigmoid(sm[rows, :] + bf_ref[...])
        fcum = _cumsum_rows(tri3_ref[...], log_f) + carry
        carry = fcum[FOX_SUB - 1:FOX_SUB, :]
        aug = (_place_lanes(fcum * LOG2E, place_ref[...]) + bias_ref[...]).astype(BF16)
        for pr in range(pairs):
            aq_ref[pr, rows, :] = aug[:, pr * LANES:(pr + 1) * LANES]
            ak_ref[pr, rows, :] = aug[:, (pairs + pr) * LANES:(pairs + pr + 1) * LANES]
    fcar_ref[...] = jnp.broadcast_to(carry, fcar_ref.shape)


def _pre_kernel(x_ref, wg_ref, wu_ref, wd_ref, g_ref, b_ref, w_ref,
                scw_ref, scb_ref, lcw_ref, lcb_ref, wa_ref, ba_ref, wx_ref, bx_ref, lam_ref, bf_ref,
                tri3_ref, place_ref, bias_ref,
                xo_ref, xs_ref, la_ref, lb_ref, lg_ref, aq_ref, ak_ref,
                q_ref, k_ref, ve_ref, vo_ref, z_ref, sm_ref,
                sbuf_ref, lbuf_ref, fcar_ref, raw_ref, *, alpha, tiles_per_seq):
    i = pl.program_id(0)
    lw = _IN_UG // 2

    @pl.when(i == 0)
    def _():
        raw_ref[...] = jnp.zeros(raw_ref.shape, F32)
        sbuf_ref[0:_TAIL, :] = jnp.zeros((_TAIL, sbuf_ref.shape[1]), F32)
        lbuf_ref[0:_TAIL, :] = jnp.zeros((_TAIL, lbuf_ref.shape[1]), F32)
        fcar_ref[...] = jnp.zeros(fcar_ref.shape, F32)

    first = (i + tiles_per_seq - 1) % tiles_per_seq == 0
    tm = raw_ref.shape[0]

    def raw(cols, anchor):
        return raw_ref[:, cols] + jnp.tile(anchor, (tm // 8, (cols.stop - cols.start) // LANES))

    def ssd_front(j):
        cols = slice(j * LANES, (j + 1) * LANES)

        def job(anchor):
            conv = _causal_conv(sbuf_ref.at[:, cols], raw(cols, anchor), scw_ref[:, cols], scb_ref[:, cols], first)
            xs_ref[:, cols] = conv * jax.nn.sigmoid(conv)
        return job

    def lru_front(anchor):
        u = _causal_conv(lbuf_ref, raw(slice(_IN_XBC, _IN_XBC + lw), anchor), lcw_ref[...], lcb_ref[...], first)
        a, b = _lru_terms(u, wa_ref, ba_ref, wx_ref, bx_ref, lam_ref)
        la_ref[...] = a
        lb_ref[...] = b
        lg_ref[...] = jax.nn.gelu(raw_ref[:, _IN_XBC + lw:_IN_XBC + 2 * lw])

    def fox_front(anchor):
        _fox_aug(raw(slice(_IN_XBC + _IN_UG, _IN_RAW), anchor), bf_ref, tri3_ref, place_ref, bias_ref,
                 fcar_ref, aq_ref, ak_ref, first)

    jobs = [ssd_front(j) for j in range(_IN_XBC // LANES)] + [lru_front, fox_front]

    x1 = _ffn_ln_tile(x_ref[...], wg_ref, wu_ref, wd_ref, g_ref, b_ref, alpha, side_work=jobs)
    xo_ref[...] = x1
    hh = _dot(x1.astype(BF16), w_ref[...])
    raw_ref[...] = hh[:, 0:_IN_RAW]
    sm_ref[...] = hh[:, _IN_XBC + _IN_UG:_IN_RAW]
    h = hh[:, _IN_RAW:]
    o = 0
    aw = _IN_QKV // 3
    q = h[:, o:o + aw]
    k = h[:, o + aw:o + 2 * aw]
    v = h[:, o + 2 * aw:o + 3 * aw]
    o += _IN_QKV
    q_ref[...] = (q * (HEAD_DIM ** -0.5 * LOG2E)).astype(BF16)
    k_ref[...] = k.astype(BF16)
    even = (lax.broadcasted_iota(jnp.int32, v.shape, 1) % LANES) < HEAD_DIM
    ve_ref[...] = jnp.where(even, v, 1.0).astype(BF16)
    vo_ref[...] = jnp.where(even, 1.0, v).astype(BF16)
    z_ref[...] = h[:, o:o + _IN_Z]


def _pre(x, params, *, layer, alpha, seq, tm=512):
    t, d = x.shape
    aw = _IN_QKV // 3
    lw = _IN_UG // 2
    pairs = ATT_HEADS // 2
    place, bias = _fox_aug_consts()
    consts = [_tri3(FOX_SUB), place, bias]
    nt = t // tm

    def cur(n):
        return pl.BlockSpec((tm, n), lambda i: (jnp.minimum(i, nt - 1), 0))

    def prev(n):
        return pl.BlockSpec((tm, n), lambda i: (jnp.maximum(i - 1, 0), 0))

    aug = pl.BlockSpec((pairs, tm, LANES), lambda i: (0, jnp.maximum(i - 1, 0), 0))
    aug_shape = jax.ShapeDtypeStruct((pairs, t, LANES), BF16)
    outs = [(d, F32, cur), (_IN_XBC, F32, prev), (lw, F32, prev), (lw, F32, prev), (lw, F32, prev),
            None, None,
            (aw, BF16, cur), (aw, BF16, cur), (aw, BF16, cur), (aw, BF16, cur),
            (_IN_Z, F32, cur), (_IN_SMALL, F32, cur)]
    return pl.pallas_call(
        functools.partial(_pre_kernel, alpha=alpha, tiles_per_seq=seq // tm),
        out_shape=[aug_shape if o is None else jax.ShapeDtypeStruct((t, o[0]), o[1]) for o in outs],
        grid=(nt + 1,),
        in_specs=[cur(d)] + [_param(c, layer) for c in params] + [_resident(c.shape) for c in consts],
        out_specs=[aug if o is None else o[2](o[0]) for o in outs],
        scratch_shapes=[pltpu.VMEM((_TAIL + tm, _IN_XBC), F32), pltpu.VMEM((_TAIL + tm, lw), F32),
                        pltpu.VMEM((8, LANES), F32), pltpu.VMEM((tm, _IN_RAW), F32)],
        compiler_params=_cparams("arbitrary"),
        name="pre",
    )(x, *params, *consts)


def _tri3(n):
    tri = np.tril(np.ones((n, n), np.float32))
    return jnp.asarray(np.concatenate([tri, tri, tri], axis=1), BF16)


def _neg_mask(n):
    return jnp.asarray(np.where(np.tril(np.ones((n, n), bool)), 0.0, -np.inf), F32)


LRU_SUB = 64


def _lru_kernel(a_ref, b_ref, g_ref, o_ref, h_ref):
    first = pl.program_id(1) == 0
    ts, w = a_ref.shape

    @pl.when(first)
    def _():
        h_ref[...] = jnp.zeros(h_ref.shape, F32)

    carry = h_ref[0:1, :]
    row = lax.broadcasted_iota(jnp.int32, (LRU_SUB, w), 0)
    for r0 in range(0, ts, LRU_SUB):
        rows = slice(r0, r0 + LRU_SUB)
        a = a_ref[rows, :]
        b = b_ref[rows, :]
        d = 1
        while d < LRU_SUB:
            keep = row >= d
            a_s = jnp.where(keep, pltpu.roll(a, d, axis=0), 1.0)
            b_s = jnp.where(keep, pltpu.roll(b, d, axis=0), 0.0)
            b = a * b_s + b
            a = a * a_s
            d *= 2
        h = b + a * carry
        carry = h[LRU_SUB - 1:LRU_SUB, :]
        o_ref[rows, :] = (h * g_ref[rows, :]).astype(o_ref.dtype)
    h_ref[...] = jnp.broadcast_to(carry, h_ref.shape)


def _lru(a, b, g, *, batch, ts=1024):
    t, w = a.shape
    ns = t // batch // ts
    blk = pl.BlockSpec((ts, w), lambda bi, s: (bi * ns + s, 0))
    return pl.pallas_call(
        _lru_kernel,
        out_shape=jax.ShapeDtypeStruct((t, w), BF16),
        grid=(batch, ns),
        in_specs=[blk, blk, blk],
        out_specs=blk,
        scratch_shapes=[pltpu.VMEM((8, w), F32)],
        compiler_params=_cparams("arbitrary", "arbitrary"),
        name="lru",
    )(a, b, g)


def _fox_aug_consts():
    pairs = ATT_HEADS // 2
    place = np.zeros((3, LANES, 2 * pairs * LANES), np.float32)
    bias = np.zeros((1, 2 * pairs * LANES), np.float32)
    for h in range(ATT_HEADS):
        pr, e = divmod(h, 2)
        qcol = pr * LANES + AUG_STRIDE * e
        kcol = (pairs + pr) * LANES + AUG_STRIDE * e
        for term in range(3):
            place[term, h, qcol + term] = 1.0
            place[term, h, kcol + 3 + term] = -1.0
        bias[0, qcol + 3:qcol + 6] = 1.0
        bias[0, kcol:kcol + 3] = 1.0
    return jnp.asarray(place.reshape(3 * LANES, -1), BF16), jnp.asarray(bias, F32)


def _fox_kernel(q_ref, aq_ref, k_ref, ak_ref, ve_ref, vo_ref, nm_ref, o_ref,
                acc_ref, m_ref, s_ref, p_ref, al_ref, mx_ref, *, tq):
    tk = tq
    qi = pl.program_id(2)
    q2 = q_ref[...]
    aq = aq_ref[...]
    lane = lax.broadcasted_iota(jnp.int32, (tq, LANES), 1)
    low = lane < HEAD_DIM
    zero = jnp.zeros_like(q2)
    qs = []
    for e in range(2):
        own = jnp.where(low, q2, zero) if e == 0 else jnp.where(low, zero, q2)
        aug = jnp.where((lane >= AUG_STRIDE * e) & (lane < AUG_STRIDE * (e + 1)), aq, zero)
        qs.append(jnp.concatenate([own, aug], axis=1))
    v_refs = (ve_ref, vo_ref)
    reps = tk // LANES

    def rows_of(n):
        return pl.ds(pl.multiple_of(n * tk, tk), tk)

    def keys(n):
        return jnp.concatenate([k_ref[rows_of(n), :], ak_ref[rows_of(n), :]], axis=1)

    def scores(h, kb):
        return lax.dot_general(qs[h], kb, (((1,), (1,)), ((), ())), preferred_element_type=F32)

    def accumulate(h, slot, n):
        acc_ref[h] = al_ref[slot, h] * acc_ref[h] + _dot(p_ref[slot, h], v_refs[h][rows_of(n), :])

    def produce(slot, h, kb):
        s = scores(h, kb)
        s_ref[slot, h] = s
        mx_ref[slot, h] = jnp.broadcast_to(jnp.max(s, axis=1, keepdims=True), (tq, LANES))

    def stage(cur, n, diag=None, lookahead=True):
        prv = 1 - cur
        if lookahead:
            kb = keys(n + 1)
        for h in range(2):
            s = s_ref[cur, h]
            m_prev = m_ref[h]
            if diag is None:
                m_new = jnp.maximum(m_prev, mx_ref[cur, h])
            else:
                s = s + nm_ref[:, diag * tk:(diag + 1) * tk]
                m_new = jnp.maximum(m_prev, jnp.max(s, axis=1, keepdims=True))
            al_ref[cur, h] = jnp.exp2(m_prev - m_new)
            m_ref[h] = m_new
            p_ref[cur, h] = jnp.exp2(s - jnp.tile(m_new, (1, reps))).astype(BF16)
            if lookahead:
                produce(prv, h, kb)
            accumulate(h, prv, jnp.maximum(n - 1, 0))

    acc_ref[...] = jnp.zeros(acc_ref.shape, F32)
    m_ref[...] = jnp.full(m_ref.shape, -jnp.inf, F32)
    p_ref[1] = jnp.zeros(p_ref.shape[1:], BF16)
    al_ref[1] = jnp.zeros(al_ref.shape[1:], F32)
    kb = keys(0)
    for h in range(2):
        produce(0, h, kb)

    def trips(blocks):
        def body(j, carry):
            for r in range(blocks):
                stage(r % 2, blocks * j + r)
            return carry
        return body

    pairs = qi // 2
    n8 = pairs // 4
    n4 = (pairs % 4) // 2
    lax.fori_loop(0, n8, trips(8), 0)
    lax.fori_loop(2 * n8, 2 * n8 + n4, trips(4), 0)
    lax.fori_loop(4 * n8 + 2 * n4, pairs, trips(2), 0)

    @pl.when(qi % 2 == 0)
    def _():
        stage(0, qi, diag=0, lookahead=False)
        for h in range(2):
            accumulate(h, 0, qi)

    @pl.when(qi % 2 == 1)
    def _():
        stage(0, qi - 1)
        stage(1, qi, diag=0, lookahead=False)
        for h in range(2):
            accumulate(h, 1, qi)

    num = jnp.where(low, acc_ref[0], acc_ref[1])
    den = pltpu.roll(jnp.where(low, acc_ref[1], acc_ref[0]), HEAD_DIM, axis=1)
    o_ref[...] = (num / den).astype(o_ref.dtype)


def _fox(q, k, ve, vo, aq, ak, *, batch, tq=512):
    t, aw = q.shape
    s = t // batch
    nq = s // tq
    pairs = aw // LANES
    nm = _neg_mask(tq)
    qblk = pl.BlockSpec((tq, LANES), lambda b, hp, i: (b * nq + i, hp))
    kv = pl.BlockSpec((s, LANES), lambda b, hp, i: (b, hp))
    return pl.pallas_call(
        functools.partial(_fox_kernel, tq=tq),
        out_shape=jax.ShapeDtypeStruct((t, aw), BF16),
        grid=(batch, pairs, nq),
        in_specs=[qblk,
                  pl.BlockSpec((None, tq, LANES), lambda b, hp, i: (hp, b * nq + i, 0)),
                  kv,
                  pl.BlockSpec((None, s, LANES), lambda b, hp, i: (hp, b, 0)),
                  kv, kv, _resident(nm.shape)],
        out_specs=qblk,
        scratch_shapes=[pltpu.VMEM((2, tq, LANES), F32), pltpu.VMEM((2, tq, LANES), F32),
                        pltpu.VMEM((2, 2, tq, tq), F32), pltpu.VMEM((2, 2, tq, tq), BF16),
                        pltpu.VMEM((2, 2, tq, LANES), F32), pltpu.VMEM((2, 2, tq, LANES), F32)],
        compiler_params=_cparams("parallel", "parallel", "arbitrary"),
        name="fox",
    )(q, aq, k, ak, ve, vo, nm)


def _ssd_kernel(xs_ref, z_ref, sm_ref, dtb_ref, alog_ref, dskip_ref, ng_ref,
                tri3_ref, nm_ref, exp3_ref, o_ref, dt_ref, a_ref, state_ref):
    first = pl.program_id(1) == 0
    ts = xs_ref.shape[0]
    width = z_ref.shape[1]
    gw = width // SSD_GROUPS
    L = SSD_CHUNK
    nst = SSD_GROUPS * SSD_STATE

    @pl.when(first)
    def _():
        state_ref[...] = jnp.zeros(state_ref.shape, F32)

    dt = jax.nn.softplus(sm_ref[...] + dtb_ref[...])
    dt_ref[...] = dt
    a_ref[...] = -jnp.exp(alog_ref[...]) * dt

    low = lax.broadcasted_iota(jnp.int32, (L, LANES), 1) < HEAD_DIM

    def chunk(c):
        rows = pl.ds(c * L, L)
        xs = xs_ref[rows, 0:width]
        bm = xs_ref[rows, width:width + nst].astype(BF16)
        cm = xs_ref[rows, width + nst:width + 2 * nst].astype(BF16)
        a_cum = _cumsum_rows(tri3_ref[...], a_ref[rows, :])
        a_cum_t = a_cum.T
        a_exp = _place_lanes(a_cum, exp3_ref[...])
        xdt = xs * _place_lanes(dt_ref[rows, :], exp3_ref[...])
        decay_in = jnp.exp(a_exp)
        a_last = a_exp[L - 1:L, :]
        xdec = (jnp.exp(a_last - a_exp) * xdt).astype(BF16)
        chunk_decay = jnp.exp(a_last)
        zero = jnp.zeros((L, LANES), F32)
        y_parts = []
        for g in range(SSD_GROUPS):
            bg = bm[:, g * SSD_STATE:(g + 1) * SSD_STATE]
            cg = cm[:, g * SSD_STATE:(g + 1) * SSD_STATE]
            gmat = lax.dot_general(cg, bg, (((1,), (1,)), ((), ())), preferred_element_type=F32)
            for pr in range(gw // LANES):
                col = g * gw + pr * LANES
                hsl = slice(col, col + LANES)
                hd = col // HEAD_DIM
                blk = a_exp[:, hsl]
                swp = pltpu.roll(blk, HEAD_DIM, axis=1)
                cols = (jnp.where(low, blk, swp), jnp.where(low, swp, blk))
                ms = []
                for e in range(2):
                    r = DT_LANE0 + hd + e
                    seg = cols[e] - a_cum_t[r:r + 1, :]
                    ms.append((gmat * jnp.exp(seg + nm_ref[...])).astype(BF16))
                xp = xdt[:, hsl]
                rhs = jnp.concatenate([jnp.where(low, xp, zero).astype(BF16),
                                       jnp.where(low, zero, xp).astype(BF16)], axis=0)
                y_diag = _dot(jnp.concatenate(ms, axis=1), rhs)
                st = state_ref[:, hsl]
                y_off = _dot(cg, st.astype(BF16)) * decay_in[:, hsl]
                new = lax.dot_general(bg, xdec[:, hsl], (((0,), (0,)), ((), ())),
                                      preferred_element_type=F32)
                state_ref[:, hsl] = chunk_decay[:, hsl] * st + new
                y_parts.append(y_diag + y_off)
        y = jnp.concatenate(y_parts, axis=1) + dskip_ref[...] * xs
        zc = z_ref[rows, :]
        y = y * (zc * jax.nn.sigmoid(zc))
        outs = []
        for g in range(SSD_GROUPS):
            yg = y[:, g * gw:(g + 1) * gw]
            ms_ = jnp.mean(yg * yg, axis=-1, keepdims=True)
            outs.append(yg * lax.rsqrt(ms_ + RMS_EPS))
        o_ref[rows, :] = (jnp.concatenate(outs, axis=1) * ng_ref[...]).astype(o_ref.dtype)

    for c in range(ts // L):
        chunk(c)


def _ssd(xs, z, small, params, *, layer, batch, ts=1024):
    t, cdim = xs.shape
    width = z.shape[1]
    ns = t // batch // ts
    tri3 = _tri3(SSD_CHUNK)
    nm = _neg_mask(SSD_CHUNK)
    src = np.arange(LANES)[:, None]
    expand = (src == DT_LANE0 + np.arange(width)[None, :] // HEAD_DIM).astype(np.float32)
    exp3 = jnp.asarray(np.concatenate([expand] * 3, axis=0), BF16)

    def row(n):
        return pl.BlockSpec((ts, n), lambda b, s: (b * ns + s, 0))

    consts = [tri3, nm, exp3]
    return pl.pallas_call(
        _ssd_kernel,
        out_shape=jax.ShapeDtypeStruct((t, width), BF16),
        grid=(batch, ns),
        in_specs=([row(cdim), row(width), row(LANES)] + [_param(c, layer) for c in params]
                  + [_resident(c.shape) for c in consts]),
        out_specs=row(width),
        scratch_shapes=[pltpu.VMEM((ts, LANES), F32), pltpu.VMEM((ts, LANES), F32),
                        pltpu.VMEM((SSD_STATE, width), F32)],
        compiler_params=_cparams("arbitrary", "arbitrary"),
        name="ssd",
    )(xs, z, small, *params, *consts)


def _post_kernel(x_ref, ya_ref, yb_ref, yc_ref, p_ref, wo_ref, g2_ref, b2_ref, wg_ref, wu_ref, wd_ref,
                 g3_ref, b3_ref, pgw_ref, pgb_ref, ppw_ref, o_ref, *, alpha):
    na, nb = ya_ref.shape[1], yb_ref.shape[1]
    mix = (_dot(ya_ref[...], wo_ref[0:na, :]) + _dot(yb_ref[...], wo_ref[na:na + nb, :])
           + _dot(yc_ref[...], wo_ref[na + nb:, :]))
    x2 = _layer_norm(alpha * x_ref[...] + mix, g2_ref[...], b2_ref[...])
    x3 = _ffn_ln_tile(x2, wg_ref, wu_ref, wd_ref, g3_ref, b3_ref, alpha)
    gate = jax.nn.sigmoid(_dot(x3.astype(BF16), pgw_ref[...]) + pgb_ref[...])
    o_ref[...] = x3 + gate * _dot(p_ref[...].astype(BF16), ppw_ref[...])


def _post(x, ya, yb, yc, p, params, *, layer, alpha, tm=512):
    t, d = x.shape

    def row(n):
        return pl.BlockSpec((tm, n), lambda i: (i, 0))

    return pl.pallas_call(
        functools.partial(_post_kernel, alpha=alpha),
        out_shape=jax.ShapeDtypeStruct((t, d), F32),
        grid=(t // tm,),
        in_specs=([row(d), row(ya.shape[1]), row(yb.shape[1]), row(yc.shape[1]),
                   pl.BlockSpec((None, tm, p.shape[2]), lambda i: (layer, i, 0))]
                  + [_param(c, layer) for c in params]),
        out_specs=row(d),
        compiler_params=_cparams("parallel"),
        name="post",
    )(x, ya, yb, yc, p, *params)


def _block_diag(w):
    d, h, i, j = w.shape
    eye = jnp.eye(h, dtype=w.dtype)
    return (eye[None, :, None, :, None] * w[:, :, :, None, :]).reshape(d, h * i, h * j)


def _rows(v):
    return v[:, None, :].astype(F32)


def _head_lanes(v, offset):
    d, h = v.shape
    return jnp.zeros((d, 1, LANES), F32).at[:, 0, offset:offset + h].set(v.astype(F32))


def kernel(x, p, ln1_g, ln1_b, ffn1_wg, ffn1_wu, ffn1_wd, w_in, lru_conv_w, lru_conv_b, lru_wa, lru_ba, lru_wx, lru_bx, lru_lambda, fox_bf, ssd_conv_w, ssd_conv_b, ssd_dt_bias, ssd_a_log, ssd_d, ssd_norm_g, w_out, ln2_g, ln2_b, ffn2_wg, ffn2_wu, ffn2_wd, ln3_g, ln3_b, pe_proj, pe_gate_w, pe_gate_b):
    batch, seq, d_model = x.shape
    depth = p.shape[0]
    t = batch * seq
    alpha = (2.0 * depth) ** 0.25
    lru_w = lru_conv_w.shape[-1]
    att_w = ATT_HEADS * HEAD_DIM
    ssd_w = SSD_HEADS * HEAD_DIM
    conv_dim = ssd_conv_w.shape[-1]

    sizes = (lru_w, lru_w, att_w, att_w, att_w, ATT_HEADS, ssd_w, conv_dim, SSD_HEADS)
    offs = [0]
    for s_ in sizes:
        offs.append(offs[-1] + s_)
    col = lambda j: w_in[:, :, offs[j]:offs[j + 1]]
    pad = jnp.zeros((depth, d_model, LANES - ATT_HEADS - SSD_HEADS), F32)
    w_cat = jnp.concatenate([col(7), col(0), col(1), col(5), col(8), pad,
                             col(2), col(3), col(4), col(6)], axis=2).astype(BF16)

    pre_params = [ffn1_wg.astype(BF16), ffn1_wu.astype(BF16), ffn1_wd.astype(BF16),
                  _rows(ln1_g), _rows(ln1_b), w_cat,
                  ssd_conv_w, _rows(ssd_conv_b), lru_conv_w, _rows(lru_conv_b),
                  _block_diag(lru_wa).astype(BF16), _rows(lru_ba),
                  _block_diag(lru_wx).astype(BF16), _rows(lru_bx), _rows(lru_lambda),
                  _head_lanes(fox_bf, 0)]
    ssd_params = [_head_lanes(ssd_dt_bias, DT_LANE0), _head_lanes(ssd_a_log, DT_LANE0),
                  _rows(jnp.repeat(ssd_d, HEAD_DIM, axis=1)), _rows(ssd_norm_g)]
    post_params = [w_out.astype(BF16), _rows(ln2_g), _rows(ln2_b),
                   ffn2_wg.astype(BF16), ffn2_wu.astype(BF16), ffn2_wd.astype(BF16),
                   _rows(ln3_g), _rows(ln3_b),
                   pe_gate_w.astype(BF16), _rows(pe_gate_b), pe_proj.astype(BF16)]
    pf = p.reshape(depth, t, p.shape[-1])

    xf = x.reshape(t, d_model)
    for i in range(depth):
        xf, xs, la, lb, lg, aq, ak, q, k, ve, vo, z, small = _pre(xf, pre_params, layer=i, alpha=alpha,
                                                                  seq=seq)
        y_a = _lru(la, lb, lg, batch=batch)
        y_b = _fox(q, k, ve, vo, aq, ak, batch=batch)
        y_c = _ssd(xs, z, small, ssd_params, layer=i, batch=batch)
        xf = _post(xf, y_a, y_b, y_c, pf, post_params, layer=i, alpha=alpha)
    return xf.reshape(batch, seq, d_model)
```
